```python
import math
import jax, jax.numpy as jnp
from jax import lax
import numpy as np

D_MODEL = 2048
BATCH = 2
SEQ = 16384
DEPTH = 1
DEC_BATCH = 32
DEC_SEQ = 64
PAST_LEN = 1024

CHUNK = 64
N_META = 16
Q_BLOCK = 128
GLA_HEADS = 4
GLA_DK = D_MODEL // (2 * GLA_HEADS)
GLA_DV = D_MODEL // GLA_HEADS
GATE_RANK = 16
GLA_TAU = 16.0
DIFF_HEADS = 8
DIFF_DQK = D_MODEL // (2 * DIFF_HEADS)
DIFF_DV = 2 * DIFF_DQK
ROPE_DIM = DIFF_DQK // 4
ROPE_THETA = 500000.0
D_FF = 4 * D_MODEL
EPS = 1e-6
SPLIT_SIZES = (GLA_HEADS * GLA_DK, GLA_HEADS * GLA_DK, GLA_HEADS * GLA_DV, GATE_RANK, GLA_HEADS * GLA_DV,
               2 * DIFF_HEADS * DIFF_DQK, 2 * DIFF_HEADS * DIFF_DQK, DIFF_HEADS * DIFF_DV, D_MODEL, D_MODEL)
IN_COLS = sum(SPLIT_SIZES)

kernel_name = 'hybrid_gla_diffattn_streaming_step'


def rmsnorm(x, g):
    xf = x.astype(jnp.float32)
    y = xf * lax.rsqrt(jnp.mean(jnp.square(xf), axis=-1, keepdims=True) + EPS)
    return (y * g.astype(jnp.float32)).astype(x.dtype)


def rope(x, pos):
    half = ROPE_DIM // 2
    inv_freq = jnp.power(ROPE_THETA, -jnp.arange(0, ROPE_DIM, 2, dtype=jnp.float32) / ROPE_DIM)
    ang = pos[:, None] * inv_freq[None, :]
    cos = jnp.cos(ang)[:, None, None, :]
    sin = jnp.sin(ang)[:, None, None, :]
    xf = x.astype(jnp.float32)
    x1 = xf[..., :half]
    x2 = xf[..., half:ROPE_DIM]
    out = jnp.concatenate([x1 * cos - x2 * sin, x2 * cos + x1 * sin, xf[..., ROPE_DIM:]], axis=-1)
    return out.astype(x.dtype)


def in_projection(u, lp):
    B, T = u.shape[:2]
    z = u @ lp['w_in']
    parts = []
    off = 0
    for size in SPLIT_SIZES:
        parts.append(z[..., off:off + size])
        off += size
    gq, gk, gv, alow, r, dq, dk, dv, ga, gb = parts
    gq = gq.reshape(B, T, GLA_HEADS, GLA_DK) * (GLA_DK ** -0.5)
    gk = gk.reshape(B, T, GLA_HEADS, GLA_DK)
    gv = gv.reshape(B, T, GLA_HEADS, GLA_DV)
    log_a = jax.nn.log_sigmoid((alow @ lp['w_gla_a2'] + lp['b_gla_a']).astype(jnp.float32)) / GLA_TAU
    log_a = log_a.reshape(B, T, GLA_HEADS, GLA_DK)
    dq = dq.reshape(B, T, DIFF_HEADS, 2, DIFF_DQK)
    dk = dk.reshape(B, T, DIFF_HEADS, 2, DIFF_DQK)
    dv = dv.reshape(B, T, DIFF_HEADS, DIFF_DV)
    return gq, gk, gv, log_a, r, dq, dk, dv, ga, gb


def gla_chunk(S, q, k, v, log_a):
    T = q.shape[1]
    b = jnp.cumsum(log_a, axis=1)
    qf = q.astype(jnp.float32)
    kf = k.astype(jnp.float32)
    vf = v.astype(jnp.float32)
    Sf = S.astype(jnp.float32)
    inter = jnp.einsum('bthc,bhcv->bthv', qf * jnp.exp(b), Sf)
    causal = jnp.tril(jnp.ones((T, T), dtype=bool))[None, :, :, None, None]
    rel = jnp.where(causal, b[:, :, None] - b[:, None], -jnp.inf)
    scores = jnp.einsum('bthc,bshc,btshc->bhts', qf, kf, jnp.exp(rel))
    intra = jnp.einsum('bhts,bshv->bthv', scores, vf)
    b_last = b[:, -1]
    k_dec = kf * jnp.exp(b_last[:, None] - b)
    S_new = jnp.exp(b_last)[..., None] * Sf + jnp.einsum('bshc,bshv->bhcv', k_dec, vf)
    return (inter + intra).astype(v.dtype), S_new.astype(S.dtype)


def diff_lambda(lp, lam_init):
    f = jnp.float32
    return (jnp.exp(jnp.sum(lp['diff_lq1'].astype(f) * lp['diff_lk1'].astype(f)))
            - jnp.exp(jnp.sum(lp['diff_lq2'].astype(f) * lp['diff_lk2'].astype(f))) + lam_init)


def diff_attend(q, k, v, mask, lam):
    s = jnp.einsum('bqhmd,bkhmd->bhmqk', q, k).astype(jnp.float32) * (DIFF_DQK ** -0.5)
    s = jnp.where(mask, s, -jnp.inf)
    p = jax.nn.softmax(s, axis=-1)
    pd = p[:, :, 0] - lam * p[:, :, 1]
    return jnp.einsum('bhqk,bkhv->bqhv', pd.astype(v.dtype), v)


def merge_and_ffn(x, o_gla, r, o_diff, ga, gb, lp, lam_init):
    B, T = x.shape[:2]
    a = rmsnorm(o_gla, lp['gla_norm']).reshape(B, T, GLA_HEADS * GLA_DV) * jax.nn.silu(r)
    d = (rmsnorm(o_diff, lp['diff_norm']) * (1.0 - lam_init)).reshape(B, T, DIFF_HEADS * DIFF_DV)
    mix = jax.nn.sigmoid(ga) * (a @ lp['w_br_gla']) + jax.nn.sigmoid(gb) * (d @ lp['w_br_diff'])
    h = x + rmsnorm(mix @ lp['w_o'], lp['norm_mix_post'])
    m = jnp.square(jax.nn.relu(rmsnorm(h, lp['norm_ffn_pre']) @ lp['w_ff1'])) @ lp['w_ff2']
    return h + rmsnorm(m, lp['norm_ffn_post'])


def prompt_layer(x, lp, lam_init):
    Bp, L = x.shape[:2]
    n_real = L - N_META
    u = rmsnorm(x, lp['norm_mix_pre'])
    gq, gk, gv, log_a, r, dq, dk, dv, ga, gb = in_projection(u, lp)
    S0 = jnp.zeros((Bp, GLA_HEADS, GLA_DK, GLA_DV), dtype=x.dtype)
    o_meta, S = gla_chunk(S0, gq[:, :N_META], gk[:, :N_META], gv[:, :N_META], log_a[:, :N_META])

    def to_chunks(t):
        return jnp.moveaxis(t[:, N_META:].reshape((Bp, n_real // CHUNK, CHUNK) + t.shape[2:]), 1, 0)

    def step(S_c, xs):
        o, S_n = gla_chunk(S_c, *xs)
        return S_n, o

    S_fin, o_real = lax.scan(step, S, (to_chunks(gq), to_chunks(gk), to_chunks(gv), to_chunks(log_a)))
    o_real = jnp.moveaxis(o_real, 0, 1).reshape(Bp, n_real, GLA_HEADS, GLA_DV)
    o_gla = jnp.concatenate([o_meta, o_real], axis=1)
    pos = jnp.arange(L, dtype=jnp.float32)
    dq = rope(dq, pos)
    dk = rope(dk, pos)
    lam = diff_lambda(lp, lam_init)
    o_dmeta = diff_attend(dq[:, :N_META], dk[:, :N_META], dv[:, :N_META],
                          jnp.ones((N_META, N_META), dtype=bool), lam)
    key_chunk = jnp.concatenate([-jnp.ones((N_META,), jnp.int32), jnp.arange(n_real, dtype=jnp.int32) // CHUNK])
    n_blk = n_real // Q_BLOCK
    q_blocks = jnp.moveaxis(dq[:, N_META:].reshape(Bp, n_blk, Q_BLOCK, DIFF_HEADS, 2, DIFF_DQK), 1, 0)

    def attend_block(args):
        qb, i = args
        q_chunk = (i * Q_BLOCK + jnp.arange(Q_BLOCK, dtype=jnp.int32)) // CHUNK
        mask = key_chunk[None, :] <= q_chunk[:, None]
        return diff_attend(qb, dk, dv, mask, lam)

    o_dreal = lax.map(attend_block, (q_blocks, jnp.arange(n_blk, dtype=jnp.int32)))
    o_dreal = jnp.moveaxis(o_dreal, 0, 1).reshape(Bp, n_real, DIFF_HEADS, DIFF_DV)
    o_diff = jnp.concatenate([o_dmeta, o_dreal], axis=1)
    y = merge_and_ffn(x, o_gla, r, o_diff, ga, gb, lp, lam_init)
    return y, dk, dv, S_fin


def sample_layer(x, cache_k_l, cache_v_l, state_l, lp, lam_init):
    T = x.shape[1]
    P = cache_k_l.shape[1]
    u = rmsnorm(x, lp['norm_mix_pre'])
    gq, gk, gv, log_a, r, dq, dk, dv, ga, gb = in_projection(u, lp)
    o_gla, S_new = gla_chunk(state_l, gq, gk, gv, log_a)
    pos = jnp.arange(T, dtype=jnp.float32) + P
    dq = rope(dq, pos)
    dk = rope(dk, pos)
    lam = diff_lambda(lp, lam_init)
    k_all = jnp.concatenate([cache_k_l.astype(dk.dtype), dk], axis=1)
    v_all = jnp.concatenate([cache_v_l.astype(dv.dtype), dv], axis=1)
    o_diff = diff_attend(dq, k_all, v_all, jnp.ones((T, P + T), dtype=bool), lam)
    y = merge_and_ffn(x, o_gla, r, o_diff, ga, gb, lp, lam_init)
    return y, dk, dv, S_new


def setup_inputs(seed: int = 0) -> dict:
    key = jax.random.key(seed)
    ks = jax.random.split(key, 24)
    f32 = jnp.float32

    def nrm(k, shape, scale=1.0):
        return jax.random.normal(k, shape, f32) * scale

    def gain(k, n):
        return 1.0 + 0.02 * jax.random.normal(k, (DEPTH, n), f32)

    return {
        'x_prompt': nrm(ks[0], (BATCH, SEQ, D_MODEL)),
        'x_sample': nrm(ks[1], (DEC_BATCH, DEC_SEQ, D_MODEL)),
        'cache_k': nrm(ks[2], (DEPTH, DEC_BATCH, PAST_LEN, DIFF_HEADS, 2, DIFF_DQK)),
        'cache_v': nrm(ks[3], (DEPTH, DEC_BATCH, PAST_LEN, DIFF_HEADS, DIFF_DV)),
        'state_gla': nrm(ks[4], (DEPTH, DEC_BATCH, GLA_HEADS, GLA_DK, GLA_DV), 0.5),
        'meta': nrm(ks[5], (N_META, D_MODEL)),
        'norm_mix_pre': gain(ks[6], D_MODEL),
        'w_in': nrm(ks[7], (DEPTH, D_MODEL, IN_COLS), D_MODEL ** -0.5),
        'w_gla_a2': nrm(ks[8], (DEPTH, GATE_RANK, GLA_HEADS * GLA_DK), GATE_RANK ** -0.5),
        'b_gla_a': nrm(ks[9], (DEPTH, GLA_HEADS * GLA_DK), 0.1),
        'gla_norm': gain(ks[10], GLA_DV),
        'diff_lq1': nrm(ks[11], (DEPTH, DIFF_DQK), 0.1),
        'diff_lk1': nrm(ks[12], (DEPTH, DIFF_DQK), 0.1),
        'diff_lq2': nrm(ks[13], (DEPTH, DIFF_DQK), 0.1),
        'diff_lk2': nrm(ks[14], (DEPTH, DIFF_DQK), 0.1),
        'diff_norm': gain(ks[15], DIFF_DV),
        'w_br_gla': nrm(ks[16], (DEPTH, GLA_HEADS * GLA_DV, D_MODEL), (GLA_HEADS * GLA_DV) ** -0.5),
        'w_br_diff': nrm(ks[17], (DEPTH, DIFF_HEADS * DIFF_DV, D_MODEL), (DIFF_HEADS * DIFF_DV) ** -0.5),
        'w_o': nrm(ks[18], (DEPTH, D_MODEL, D_MODEL), D_MODEL ** -0.5),
        'norm_mix_post': gain(ks[19], D_MODEL),
        'norm_ffn_pre': gain(ks[20], D_MODEL),
        'w_ff1': nrm(ks[21], (DEPTH, D_MODEL, D_FF), D_MODEL ** -0.5),
        'w_ff2': nrm(ks[22], (DEPTH, D_FF, D_MODEL), D_FF ** -0.5),
        'norm_ffn_post': gain(ks[23], D_MODEL),
    }


def reference(x_prompt, x_sample, cache_k, cache_v, state_gla, meta, norm_mix_pre, w_in, w_gla_a2,
              b_gla_a, gla_norm, diff_lq1, diff_lk1, diff_lq2, diff_lk2, diff_norm, w_br_gla, w_br_diff,
              w_o, norm_mix_post, norm_ffn_pre, w_ff1, w_ff2, norm_ffn_post):
    Bp = x_prompt.shape[0]
    hp = jnp.concatenate([jnp.broadcast_to(meta[None].astype(x_prompt.dtype), (Bp, N_META, D_MODEL)),
                          x_prompt], axis=1)
    hs = x_sample
    kp_list, vp_list, sp_list, ks_list, vs_list, ss_list = [], [], [], [], [], []
    for l in range(DEPTH):
        lp = {
            'norm_mix_pre': norm_mix_pre[l], 'w_in': w_in[l], 'w_gla_a2': w_gla_a2[l], 'b_gla_a': b_gla_a[l],
            'gla_norm': gla_norm[l], 'diff_lq1': diff_lq1[l], 'diff_lk1': diff_lk1[l],
            'diff_lq2': diff_lq2[l], 'diff_lk2': diff_lk2[l], 'diff_norm': diff_norm[l],
            'w_br_gla': w_br_gla[l], 'w_br_diff': w_br_diff[l], 'w_o': w_o[l],
            'norm_mix_post': norm_mix_post[l], 'norm_ffn_pre': norm_ffn_pre[l],
            'w_ff1': w_ff1[l], 'w_ff2': w_ff2[l], 'norm_ffn_post': norm_ffn_post[l],
        }
        lam_init = 0.8 - 0.6 * math.exp(-0.3 * l)
        hp, kp, vp, sp = prompt_layer(hp, lp, lam_init)
        hs, kn, vn, sn = sample_layer(hs, cache_k[l], cache_v[l], state_gla[l], lp, lam_init)
        kp_list.append(kp)
        vp_list.append(vp)
        sp_list.append(sp)
        ks_list.append(kn)
        vs_list.append(vn)
        ss_list.append(sn)
    y_prompt = hp[:, N_META:]
    y_sample = hs
    new_k_prompt = jnp.stack(kp_list, axis=0)
    new_v_prompt = jnp.stack(vp_list, axis=0)
    new_state_gla_prompt = jnp.stack(sp_list, axis=0)
    new_k_sample = jnp.stack(ks_list, axis=0)
    new_v_sample = jnp.stack(vs_list, axis=0)
    new_state_gla_sample = jnp.stack(ss_list, axis=0)
    return (y_prompt, y_sample, new_k_prompt, new_v_prompt, new_state_gla_prompt,
            new_k_sample, new_v_sample, new_state_gla_sample)
```

```python
import functools
import math

import jax
import jax.numpy as jnp
from jax import lax
from jax.experimental import pallas as pl
from jax.experimental.pallas import tpu as pltpu

F32 = jnp.float32
BF16 = jnp.bfloat16

EPS = 1e-6
CHUNK = 64
SUB = 16
N_SUB = CHUNK // SUB
N_META = 16
META_ROWS = CHUNK
META_LO = META_ROWS - N_META
GLA_HEADS = 4
GLA_DK = 256
GLA_DV = 512
GATE_RANK = 16
GLA_TAU = 16.0
DIFF_HEADS = 8
DIFF_DQK = 128
DIFF_DV = 256
ROPE_DIM = 32
ROPE_THETA = 500000.0
LANE = 128
MASKED = -1e30
VMEM_LIMIT = 56 * 1024 * 1024

_NT = (((1,), (1,)), ((), ()))
_TN = (((0,), (0,)), ((), ()))


def _tile(n, pref):
    return pref if n % pref == 0 else n


def _params(sem):
    return pltpu.CompilerParams(dimension_semantics=sem, vmem_limit_bytes=VMEM_LIMIT)


def _rms(x):
    return x * lax.rsqrt(jnp.mean(x * x, axis=-1, keepdims=True) + EPS)


def _rope(z, cos, sin):
    half = ROPE_DIM // 2
    lane = lax.broadcasted_iota(jnp.int32, (z.shape[0], LANE), 1)
    out = []
    for g in range(z.shape[1] // LANE):
        seg = z[:, g * LANE:(g + 1) * LANE]
        partner = jnp.where(lane < half, pltpu.roll(seg, LANE - half, 1), pltpu.roll(seg, half, 1))
        out.append(seg * cos + partner * sin)
    return jnp.concatenate(out, axis=1)


def _proj_kernel(*refs, mode):
    if mode == "gla":
        x_ref, g_ref, w_ref, wal_ref, wa2_ref, ba_ref, z_ref, la_ref, u_ref = refs
    elif mode == "q":
        x_ref, g_ref, w_ref, cos_ref, sin_ref, ob_ref, u_ref = refs
    elif mode == "k":
        x_ref, g_ref, w_ref, cos_ref, sin_ref, of_ref, ob_ref, u_ref = refs
    elif mode == "v":
        x_ref, g_ref, w_ref, of_ref, ob_ref, u_ref = refs
    else:
        x_ref, g_ref, w_ref, of_ref, u_ref = refs

    @pl.when(pl.program_id(1) == 0)
    def _():
        u = (_rms(x_ref[...]) * g_ref[...]).astype(BF16)
        u_ref[...] = u
        if mode == "gla":
            alow = jnp.dot(u, wal_ref[...], preferred_element_type=F32)
            pre = jnp.dot(alow.astype(BF16), wa2_ref[...], preferred_element_type=F32) + ba_ref[...]
            la_ref[...] = (jnp.minimum(pre, 0.0) - jnp.log1p(jnp.exp(-jnp.abs(pre)))) * (1.0 / GLA_TAU)

    z = jnp.dot(u_ref[...], w_ref[...], preferred_element_type=F32)
    if mode == "gla":
        z_ref[...] = z
    elif mode == "q":
        ob_ref[...] = _rope(z, cos_ref[...], sin_ref[...]).astype(BF16)
    elif mode == "k":
        zr = _rope(z, cos_ref[...], sin_ref[...])
        of_ref[...] = zr
        ob_ref[...] = zr.astype(BF16)
    elif mode == "v":
        of_ref[...] = z
        ob_ref[...] = z.astype(BF16)
    else:
        of_ref[...] = 1.0 / (1.0 + jnp.exp(-z))


def _proj_call(mode, x, g, w, extra=(), tm_pref=1024, tn_pref=512):
    rows, d = x.shape
    n = w.shape[1]
    tm, tn = _tile(rows, tm_pref), _tile(n, tn_pref)
    row_blk = lambda i, j: (i, 0)
    fixed = lambda i, j: (0, 0)
    in_specs = [pl.BlockSpec((tm, d), row_blk), pl.BlockSpec((1, d), fixed), pl.BlockSpec((d, tn), lambda i, j: (0, j))]
    out_blk = pl.BlockSpec((tm, tn), lambda i, j: (i, j))
    if mode == "gla":
        wal, wa2, ba = extra
        in_specs += [pl.BlockSpec(wal.shape, fixed), pl.BlockSpec(wa2.shape, fixed), pl.BlockSpec(ba.shape, fixed)]
        out_shape = (jax.ShapeDtypeStruct((rows, n), F32), jax.ShapeDtypeStruct((rows, wa2.shape[1]), F32))
        out_specs = (out_blk, pl.BlockSpec((tm, wa2.shape[1]), row_blk))
    elif mode in ("q", "k"):
        in_specs += [pl.BlockSpec((tm, LANE), row_blk), pl.BlockSpec((tm, LANE), row_blk)]
        if mode == "q":
            out_shape, out_specs = jax.ShapeDtypeStruct((rows, n), BF16), out_blk
        else:
            out_shape = (jax.ShapeDtypeStruct((rows, n), F32), jax.ShapeDtypeStruct((rows, n), BF16))
            out_specs = (out_blk, out_blk)
    elif mode == "v":
        out_shape = (jax.ShapeDtypeStruct((rows, n), F32), jax.ShapeDtypeStruct((rows, n), BF16))
        out_specs = (out_blk, out_blk)
    else:
        out_shape, out_specs = jax.ShapeDtypeStruct((rows, n), F32), out_blk
    return pl.pallas_call(
        functools.partial(_proj_kernel, mode=mode),
        grid=(rows // tm, n // tn),
        in_specs=in_specs,
        out_specs=out_specs,
        out_shape=out_shape,
        scratch_shapes=[pltpu.VMEM((tm, d), BF16)],
        compiler_params=_params(("parallel", "arbitrary")),
        name="proj_" + mode,
    )(x, g, w, *extra)


def _gla_kernel(q_ref, k_ref, v_ref, r_ref, la_ref, s0_ref, gn_ref, a_ref, sout_ref, st_ref, *, n_chunks):
    c = pl.program_id(2)

    @pl.when(c == 0)
    def _():
        st_ref[...] = s0_ref[...].T

    rsub = lax.broadcasted_iota(jnp.int32, (CHUNK, GLA_DK), 0) % SUB
    rr = lax.broadcasted_iota(jnp.int32, (CHUNK, N_SUB * CHUNK), 0)
    cc = lax.broadcasted_iota(jnp.int32, (CHUNK, N_SUB * CHUNK), 1)
    valid = jnp.where(cc // CHUNK == rr // SUB, cc % CHUNK, CHUNK) <= rr
    gn = gn_ref[...]
    zero = jnp.zeros((SUB, GLA_DK), F32)

    def prod(xs):
        out = xs[0]
        for x in xs[1:]:
            out = out * x
        return out

    def chunk(ci, carry):
        rows = pl.ds(pl.multiple_of(ci * CHUNK, CHUNK), CHUNK)
        q = q_ref[rows, :] * (GLA_DK ** -0.5)
        k = k_ref[rows, :]
        vb = v_ref[rows, :].astype(BF16)
        bl = la_ref[rows, :]
        for sh in (1, 2, 4, 8):
            bl = bl + jnp.where(rsub >= sh, pltpu.roll(bl, sh, 0), 0.0)
        et = [jnp.exp(bl[SUB * j + SUB - 1:SUB * (j + 1), :]) for j in range(N_SUB)]
        qt = q * jnp.exp(bl)
        kd = k * jnp.exp(-bl)
        qs = [qt[SUB * j:SUB * (j + 1)] for j in range(N_SUB)]
        kds = [kd[SUB * j:SUB * (j + 1)] for j in range(N_SUB)]
        khat = [kds[j] * et[j] for j in range(N_SUB)]

        blocks = []
        for i in range(N_SUB):
            for j in range(N_SUB):
                if j < i:
                    blocks.append(khat[j] if j + 1 == i else khat[j] * prod(et[j + 1:i]))
                else:
                    blocks.append(kds[j] if j == i else zero)
        kstack = jnp.concatenate(blocks, axis=0).astype(BF16)
        sc = lax.dot_general(qt.astype(BF16), kstack, _NT, preferred_element_type=F32)
        p = jnp.where(valid, sc, 0.0).astype(BF16)
        intra = jnp.dot(p, jnp.concatenate([vb] * N_SUB, axis=0), preferred_element_type=F32)

        qd = jnp.concatenate([qs[i] if i == 0 else qs[i] * prod(et[:i]) for i in range(N_SUB)], axis=0)
        st = st_ref[...]
        inter = lax.dot_general(qd.astype(BF16), st.astype(BF16), _NT, preferred_element_type=F32)
        o = inter + intra

        kdec = jnp.concatenate(
            [khat[j] if j == N_SUB - 1 else khat[j] * prod(et[j + 1:]) for j in range(N_SUB)], axis=0)
        st_ref[...] = st * prod(et) + lax.dot_general(vb, kdec.astype(BF16), _TN, preferred_element_type=F32)

        r = r_ref[rows, :]
        a_ref[rows, :] = (_rms(o) * gn * (r / (1.0 + jnp.exp(-r)))).astype(BF16)
        return carry

    lax.fori_loop(0, n_chunks, chunk, 0)

    @pl.when(c == pl.num_programs(2) - 1)
    def _():
        sout_ref[...] = st_ref[...].T


def _gla_call(zg, la, s0, gn, n_batch, tc_pref=512):
    rows = zg.shape[0]
    per_b = rows // n_batch
    tc = _tile(per_b, tc_pref)
    n_tc = per_b // tc
    s_stride = 0 if s0.shape[0] == 1 else 1
    row = lambda b, h, c: b * n_tc + c
    kq = GLA_HEADS * GLA_DK // GLA_DK
    kv = 2 * GLA_HEADS * GLA_DK // GLA_DV
    kr = kv + GLA_HEADS
    state_blk = (None, None, GLA_DK, GLA_DV)
    return pl.pallas_call(
        functools.partial(_gla_kernel, n_chunks=tc // CHUNK),
        grid=(n_batch, GLA_HEADS, n_tc),
        in_specs=[
            pl.BlockSpec((tc, GLA_DK), lambda b, h, c: (row(b, h, c), h)),
            pl.BlockSpec((tc, GLA_DK), lambda b, h, c: (row(b, h, c), kq + h)),
            pl.BlockSpec((tc, GLA_DV), lambda b, h, c: (row(b, h, c), kv + h)),
            pl.BlockSpec((tc, GLA_DV), lambda b, h, c: (row(b, h, c), kr + h)),
            pl.BlockSpec((tc, GLA_DK), lambda b, h, c: (row(b, h, c), h)),
            pl.BlockSpec(state_blk, lambda b, h, c: (b * s_stride, h, 0, 0)),
            pl.BlockSpec((1, GLA_DV), lambda b, h, c: (0, 0)),
        ],
        out_specs=(
            pl.BlockSpec((tc, GLA_DV), lambda b, h, c: (row(b, h, c), h)),
            pl.BlockSpec(state_blk, lambda b, h, c: (b, h, 0, 0)),
        ),
        out_shape=(
            jax.ShapeDtypeStruct((rows, GLA_HEADS * GLA_DV), BF16),
            jax.ShapeDtypeStruct((n_batch, GLA_HEADS, GLA_DK, GLA_DV), F32),
        ),
        scratch_shapes=[pltpu.VMEM((GLA_DV, GLA_DK), F32)],
        compiler_params=_params(("parallel", "parallel", "arbitrary")),
        name="gla",
    )(zg, zg, zg, zg, la, s0, gn)


def _lam(lq1, lk1, lq2, lk2, lam_init):
    return (jnp.exp(jnp.sum(lq1[...] * lk1[...], axis=1, keepdims=True))
            - jnp.exp(jnp.sum(lq2[...] * lk2[...], axis=1, keepdims=True)) + lam_init)


def _scores(q, k):
    s0 = lax.dot_general(q[:, :DIFF_DQK], k[:, :DIFF_DQK], _NT, preferred_element_type=F32)
    s1 = lax.dot_general(q[:, DIFF_DQK:], k[:, DIFF_DQK:], _NT, preferred_element_type=F32)
    return jnp.concatenate([s0, s1], axis=0) * (DIFF_DQK ** -0.5)


def _diff_out(acc, l, lam, dn, lam_init):
    tq = acc.shape[0] // 2
    o = acc[:tq] * (1.0 / l[:tq]) - lam * (acc[tq:] * (1.0 / l[tq:]))
    return (_rms(o) * dn * (1.0 - lam_init)).astype(BF16)


def _attn_prompt_kernel(q_ref, k_ref, v_ref, kp_ref, vp_ref, lq1, lk1, lq2, lk2, dn_ref, o_ref,
                        m_ref, l_ref, acc_ref, *, tq, tk, lam_init):
    qi = pl.program_id(2)
    q = q_ref[...]
    m_ref[...] = jnp.full(m_ref.shape, MASKED, F32)
    l_ref[...] = jnp.zeros(l_ref.shape, F32)
    acc_ref[...] = jnp.zeros(acc_ref.shape, F32)

    def update(s, vb):
        m_prev = m_ref[...]
        m_new = jnp.maximum(m_prev, jnp.max(s, axis=1, keepdims=True))
        alpha = jnp.exp(m_prev - m_new)
        p = jnp.exp(s - m_new)
        l_ref[...] = alpha * l_ref[...] + jnp.sum(p, axis=1, keepdims=True)
        acc_ref[...] = alpha * acc_ref[...] + jnp.dot(p.astype(BF16), vb, preferred_element_type=F32)
        m_ref[...] = m_new

    sp = _scores(q, kp_ref[...])
    pcol = lax.broadcasted_iota(jnp.int32, sp.shape, 1)
    update(jnp.where(pcol >= META_LO, sp, MASKED), vp_ref[...])

    def full_block(kb, carry):
        rows = pl.ds(pl.multiple_of(kb * tk, tk), tk)
        update(_scores(q, k_ref[rows, :]), v_ref[rows, :])
        return carry

    lax.fori_loop(0, qi * (tq // tk), full_block, 0)

    q_chunk = (lax.broadcasted_iota(jnp.int32, (2 * tq, tk), 0) % tq) // CHUNK
    k_chunk = lax.broadcasted_iota(jnp.int32, (2 * tq, tk), 1) // CHUNK
    for d in range(tq // tk):
        rows = pl.ds(pl.multiple_of(qi * tq + d * tk, tk), tk)
        s = _scores(q, k_ref[rows, :])
        update(jnp.where(k_chunk + d * (tk // CHUNK) <= q_chunk, s, MASKED), v_ref[rows, :])

    lam = _lam(lq1, lk1, lq2, lk2, lam_init)
    o_ref[...] = _diff_out(acc_ref[...], l_ref[...], lam, dn_ref[...], lam_init)


def _attn_prompt_call(q, kb, vb, kmeta, vmeta, lams, dn, n_batch, lam_init, tq_pref=512, tk_pref=512):
    rows, width = q.shape
    t = rows // n_batch
    tq = _tile(t, tq_pref)
    tk = _tile(tq, tk_pref)
    nq = t // tq
    hw = 2 * DIFF_DQK
    fixed = lambda b, h, i: (0, 0)
    kv_spec = pl.BlockSpec((t, hw), lambda b, h, i: (b, h), pipeline_mode=pl.Buffered(1))
    meta_spec = pl.BlockSpec((META_ROWS, hw), lambda b, h, i: (0, h))
    vec = pl.BlockSpec((1, DIFF_DQK), fixed)
    return pl.pallas_call(
        functools.partial(_attn_prompt_kernel, tq=tq, tk=tk, lam_init=lam_init),
        grid=(n_batch, DIFF_HEADS, nq),
        in_specs=[pl.BlockSpec((tq, hw), lambda b, h, i: (b * nq + i, h)), kv_spec, kv_spec, meta_spec, meta_spec,
                  vec, vec, vec, vec, pl.BlockSpec((1, DIFF_DV), fixed)],
        out_specs=pl.BlockSpec((tq, DIFF_DV), lambda b, h, i: (b * nq + i, h)),
        out_shape=jax.ShapeDtypeStruct((rows, width), BF16),
        scratch_shapes=[pltpu.VMEM((2 * tq, 1), F32), pltpu.VMEM((2 * tq, 1), F32), pltpu.VMEM((2 * tq, DIFF_DV), F32)],
        compiler_params=_params(("parallel", "parallel", "arbitrary")),
        name="attn_prompt",
    )(q, kb, vb, kmeta, vmeta, *lams, dn)


def _attn_sample_kernel(q_ref, kc_ref, vc_ref, kn_ref, vn_ref, lq1, lk1, lq2, lk2, dn_ref, o_ref, *, lam_init):
    q = q_ref[...]
    sc = _scores(q, kc_ref[...].astype(BF16))
    sn = _scores(q, kn_ref[...])
    m = jnp.maximum(jnp.max(sc, axis=1, keepdims=True), jnp.max(sn, axis=1, keepdims=True))
    pc = jnp.exp(sc - m)
    pn = jnp.exp(sn - m)
    l = jnp.sum(pc, axis=1, keepdims=True) + jnp.sum(pn, axis=1, keepdims=True)
    acc = (jnp.dot(pc.astype(BF16), vc_ref[...].astype(BF16), preferred_element_type=F32)
           + jnp.dot(pn.astype(BF16), vn_ref[...], preferred_element_type=F32))
    lam = _lam(lq1, lk1, lq2, lk2, lam_init)
    o_ref[...] = _diff_out(acc, l, lam, dn_ref[...], lam_init)


def _attn_sample_call(q, kcache, vcache, knew, vnew, lams, dn, lam_init):
    rows, width = q.shape
    n_batch, past = kcache.shape[:2]
    t = rows // n_batch
    hw = 2 * DIFF_DQK
    fixed = lambda b, h: (0, 0)
    new_spec = pl.BlockSpec((t, hw), lambda b, h: (b, h))
    cache_spec = pl.BlockSpec((None, past, hw), lambda b, h: (b, 0, h))
    vec = pl.BlockSpec((1, DIFF_DQK), fixed)
    return pl.pallas_call(
        functools.partial(_attn_sample_kernel, lam_init=lam_init),
        grid=(n_batch, DIFF_HEADS),
        in_specs=[new_spec, cache_spec, cache_spec, new_spec, new_spec, vec, vec, vec, vec,
                  pl.BlockSpec((1, DIFF_DV), fixed)],
        out_specs=new_spec,
        out_shape=jax.ShapeDtypeStruct((rows, width), BF16),
        compiler_params=_params(("parallel", "parallel")),
        name="attn_sample",
    )(q, kcache, vcache, knew, vnew, *lams, dn)


def _mlp2_kernel(*refs, mode):
    if mode == "merge":
        a_ref, d_ref, ga_ref, gb_ref, wa_ref, wb_ref, w2_ref, res_ref, gpost_ref, o_ref = refs
    else:
        res_ref, gpre_ref, w1_ref, w2_ref, gpost_ref, o_ref, u_ref = refs
    j = pl.program_id(1)

    if mode == "merge":
        mid = (ga_ref[...] * jnp.dot(a_ref[...], wa_ref[...], preferred_element_type=F32)
               + gb_ref[...] * jnp.dot(d_ref[...], wb_ref[...], preferred_element_type=F32))
    else:
        @pl.when(j == 0)
        def _():
            u_ref[...] = (_rms(res_ref[...]) * gpre_ref[...]).astype(BF16)

        mid = jnp.square(jnp.maximum(jnp.dot(u_ref[...], w1_ref[...], preferred_element_type=F32), 0.0))
    part = jnp.dot(mid.astype(BF16), w2_ref[...], preferred_element_type=F32)

    @pl.when(j == 0)
    def _():
        o_ref[...] = part

    @pl.when(j > 0)
    def _():
        o_ref[...] += part

    @pl.when(j == pl.num_programs(1) - 1)
    def _():
        o_ref[...] = res_ref[...] + _rms(o_ref[...]) * gpost_ref[...]


def _merge_call(a, d, gates, wa, wb, wo, res, gpost, tm_pref=512, tn_pref=512):
    rows, dm = res.shape
    k = a.shape[1]
    tm, tn = _tile(rows, tm_pref), _tile(dm, tn_pref)
    nj = dm // tn
    row_blk = lambda i, j: (i, 0)
    return pl.pallas_call(
        functools.partial(_mlp2_kernel, mode="merge"),
        grid=(rows // tm, nj),
        in_specs=[
            pl.BlockSpec((tm, k), row_blk), pl.BlockSpec((tm, k), row_blk),
            pl.BlockSpec((tm, tn), lambda i, j: (i, j)), pl.BlockSpec((tm, tn), lambda i, j: (i, nj + j)),
            pl.BlockSpec((k, tn), lambda i, j: (0, j)), pl.BlockSpec((k, tn), lambda i, j: (0, j)),
            pl.BlockSpec((tn, dm), lambda i, j: (j, 0)),
            pl.BlockSpec((tm, dm), row_blk), pl.BlockSpec((1, dm), lambda i, j: (0, 0)),
        ],
        out_specs=pl.BlockSpec((tm, dm), row_blk),
        out_shape=jax.ShapeDtypeStruct((rows, dm), F32),
        compiler_params=_params(("parallel", "arbitrary")),
        name="merge",
    )(a, d, gates, gates, wa, wb, wo, res, gpost)


def _ffn_call(h, gpre, w1, w2, gpost, tm_pref=512, tf_pref=512):
    rows, dm = h.shape
    dff = w1.shape[1]
    tm, tf = _tile(rows, tm_pref), _tile(dff, tf_pref)
    row_blk = lambda i, j: (i, 0)
    vec = pl.BlockSpec((1, dm), lambda i, j: (0, 0))
    return pl.pallas_call(
        functools.partial(_mlp2_kernel, mode="ffn"),
        grid=(rows // tm, dff // tf),
        in_specs=[pl.BlockSpec((tm, dm), row_blk), vec, pl.BlockSpec((dm, tf), lambda i, j: (0, j)),
                  pl.BlockSpec((tf, dm), lambda i, j: (j, 0)), vec],
        out_specs=pl.BlockSpec((tm, dm), row_blk),
        out_shape=jax.ShapeDtypeStruct((rows, dm), F32),
        scratch_shapes=[pltpu.VMEM((tm, dm), BF16)],
        compiler_params=_params(("parallel", "arbitrary")),
        name="ffn",
    )(h, gpre, w1, w2, gpost)


def _rope_tables(pos):
    half = ROPE_DIM // 2
    inv_freq = jnp.power(ROPE_THETA, -jnp.arange(0, ROPE_DIM, 2, dtype=F32) / ROPE_DIM)
    ang = pos[:, None] * inv_freq[None, :]
    cos, sin = jnp.cos(ang), jnp.sin(ang)
    n = pos.shape[0]
    pad = LANE - ROPE_DIM
    return (jnp.concatenate([cos, cos, jnp.ones((n, pad), F32)], axis=1),
            jnp.concatenate([-sin, sin, jnp.zeros((n, pad), F32)], axis=1))


def kernel(x_prompt, x_sample, cache_k, cache_v, state_gla, meta, norm_mix_pre, w_in, w_gla_a2, b_gla_a, gla_norm, diff_lq1, diff_lk1, diff_lq2, diff_lk2, diff_norm, w_br_gla, w_br_diff, w_o, norm_mix_post, norm_ffn_pre, w_ff1, w_ff2, norm_ffn_post):
    n_b, seq, dm = x_prompt.shape
    n_db, dec_seq, _ = x_sample.shape
    past = cache_k.shape[2]
    assert w_in.shape[0] == 1, "single-layer step only"
    assert dec_seq == CHUNK and seq % CHUNK == 0 and meta.shape[0] == N_META
    lam_init = 0.8 - 0.6 * math.exp(-0.3 * 0)

    sizes = (GLA_HEADS * GLA_DK, GLA_HEADS * GLA_DK, GLA_HEADS * GLA_DV, GATE_RANK, GLA_HEADS * GLA_DV,
             2 * DIFF_HEADS * DIFF_DQK, 2 * DIFF_HEADS * DIFF_DQK, DIFF_HEADS * DIFF_DV, dm, dm)
    offs = [0]
    for s in sizes:
        offs.append(offs[-1] + s)
    wi = w_in[0]
    col = lambda i: wi[:, offs[i]:offs[i + 1]]
    w_gla = jnp.concatenate([col(0), col(1), col(2), col(4)], axis=1).astype(BF16)
    w_alow = jnp.pad(col(3), ((0, 0), (0, LANE - GATE_RANK))).astype(BF16)
    w_a2 = jnp.pad(w_gla_a2[0], ((0, LANE - GATE_RANK), (0, 0))).astype(BF16)
    w_q, w_k, w_v = col(5).astype(BF16), col(6).astype(BF16), col(7).astype(BF16)
    w_gate = jnp.concatenate([col(8), col(9)], axis=1).astype(BF16)
    b_a = b_gla_a
    g_pre = norm_mix_pre
    wa, wb, wo = w_br_gla[0].astype(BF16), w_br_diff[0].astype(BF16), w_o[0].astype(BF16)
    w1, w2 = w_ff1[0].astype(BF16), w_ff2[0].astype(BF16)
    lams = (diff_lq1, diff_lk1, diff_lq2, diff_lk2)

    xp = x_prompt.reshape(n_b * seq, dm)
    xs = x_sample.reshape(n_db * dec_seq, dm)
    xm = jnp.concatenate([jnp.zeros((META_LO, dm), x_prompt.dtype), meta.astype(x_prompt.dtype)], axis=0)
    cos_p, sin_p = _rope_tables(jnp.tile(jnp.arange(seq, dtype=F32) + N_META, n_b))
    cos_s, sin_s = _rope_tables(jnp.tile(jnp.arange(dec_seq, dtype=F32) + past, n_db))
    cos_m, sin_m = _rope_tables(jnp.arange(META_ROWS, dtype=F32) - META_LO)

    def project(x, cos, sin, with_q):
        zg, la = _proj_call("gla", x, g_pre, w_gla, (w_alow, w_a2, b_a))
        kf, kb = _proj_call("k", x, g_pre, w_k, (cos, sin))
        vf, vb = _proj_call("v", x, g_pre, w_v)
        if not with_q:
            return zg, la, kf, kb, vf, vb
        q = _proj_call("q", x, g_pre, w_q, (cos, sin))
        gates = _proj_call("gate", x, g_pre, w_gate)
        return zg, la, kf, kb, vf, vb, q, gates

    zg_m, la_m, kf_m, kb_m, vf_m, vb_m = project(xm, cos_m, sin_m, False)
    zero_state = jnp.zeros((1, GLA_HEADS, GLA_DK, GLA_DV), F32)
    _, s_meta = _gla_call(zg_m, la_m, zero_state, gla_norm, 1)

    zg, la, kf_p, kb_p, vf_p, vb_p, q_p, gates_p = project(xp, cos_p, sin_p, True)
    a_p, s_p = _gla_call(zg, la, s_meta, gla_norm, n_b)
    d_p = _attn_prompt_call(q_p, kb_p, vb_p, kb_m, vb_m, lams, diff_norm, n_b, lam_init)
    h_p = _merge_call(a_p, d_p, gates_p, wa, wb, wo, xp, norm_mix_post)
    y_p = _ffn_call(h_p, norm_ffn_pre, w1, w2, norm_ffn_post)

    zg, la, kf_s, kb_s, vf_s, vb_s, q_s, gates_s = project(xs, cos_s, sin_s, True)
    a_s, s_s = _gla_call(zg, la, state_gla[0].astype(F32), gla_norm, n_db)
    kc = cache_k[0].reshape(n_db, past, 2 * DIFF_HEADS * DIFF_DQK)
    vc = cache_v[0].reshape(n_db, past, DIFF_HEADS * DIFF_DV)
    d_s = _attn_sample_call(q_s, kc, vc, kb_s, vb_s, lams, diff_norm, lam_init)
    h_s = _merge_call(a_s, d_s, gates_s, wa, wb, wo, xs, norm_mix_post)
    y_s = _ffn_call(h_s, norm_ffn_pre, w1, w2, norm_ffn_post)

    def with_meta(f_meta, f_real):
        m = jnp.broadcast_to(f_meta[None, META_LO:], (n_b, N_META, f_meta.shape[1]))
        return jnp.concatenate([m, f_real.reshape(n_b, seq, -1)], axis=1)

    k_shape = (DIFF_HEADS, 2, DIFF_DQK)
    v_shape = (DIFF_HEADS, DIFF_DV)
    return (y_p.reshape(n_b, seq, dm),
            y_s.reshape(n_db, dec_seq, dm),
            with_meta(kf_m, kf_p).reshape((1, n_b, N_META + seq) + k_shape),
            with_meta(vf_m, vf_p).reshape((1, n_b, N_META + seq) + v_shape),
            s_p[None],
            kf_s.reshape((1, n_db, dec_seq) + k_shape),
            vf_s.reshape((1, n_db, dec_seq) + v_shape),
            s_s[None])
```

```python
import functools
import math

import jax
import jax.numpy as jnp
from jax import lax
from jax.experimental import pallas as pl
from jax.experimental.pallas import tpu as pltpu

F32 = jnp.float32
BF16 = jnp.bfloat16

EPS = 1e-6
CHUNK = 64
SUB = 16
N_SUB = CHUNK // SUB
N_META = 16
META_ROWS = CHUNK
META_LO = META_ROWS - N_META
GLA_HEADS = 4
GLA_DK = 256
GLA_DV = 512
GATE_RANK = 16
GLA_TAU = 16.0
DIFF_HEADS = 8
DIFF_DQK = 128
DIFF_DV = 256
ROPE_DIM = 32
ROPE_THETA = 500000.0
LANE = 128
QUERY_SCALE = DIFF_DQK ** -0.5 * math.log2(math.e)
EPILOGUE_ROWS = 128
MLP_ROWS = 256
MASKED = -1e30
VMEM_LIMIT = 56 * 1024 * 1024

_NT = (((1,), (1,)), ((), ()))
_TN = (((0,), (0,)), ((), ()))


def _tile(n, pref):
    return pref if n % pref == 0 else n


def _params(sem):
    return pltpu.CompilerParams(dimension_semantics=sem, vmem_limit_bytes=VMEM_LIMIT)


def _rms(x):
    return x * lax.rsqrt(jnp.mean(x * x, axis=-1, keepdims=True) + EPS)


def _rope(z, cos, sin):
    half = ROPE_DIM // 2
    lane = lax.broadcasted_iota(jnp.int32, (z.shape[0], LANE), 1)
    out = []
    for g in range(z.shape[1] // LANE):
        seg = z[:, g * LANE:(g + 1) * LANE]
        partner = jnp.where(lane < half, pltpu.roll(seg, LANE - half, 1), pltpu.roll(seg, half, 1))
        out.append(seg * cos + partner * sin)
    return jnp.concatenate(out, axis=1)


def _norm_kernel(x_ref, g_ref, u_ref):
    u_ref[...] = (_rms(x_ref[...]) * g_ref[...]).astype(BF16)


def _norm_call(x, g, tm_pref=512):
    rows, d = x.shape
    tm = _tile(rows, tm_pref)
    return pl.pallas_call(
        _norm_kernel,
        grid=(rows // tm,),
        in_specs=[pl.BlockSpec((tm, d), lambda i: (i, 0)), pl.BlockSpec((1, d), lambda i: (0, 0))],
        out_specs=pl.BlockSpec((tm, d), lambda i: (i, 0)),
        out_shape=jax.ShapeDtypeStruct((rows, d), BF16),
        compiler_params=_params(("parallel",)),
        name="norm",
    )(x, g)


def _proj_kernel(*refs, mode):
    if mode == "gla":
        u_ref, w_ref, wal_ref, wa2_ref, ba_ref, z_ref, la_ref = refs
    elif mode == "q":
        u_ref, w_ref, cos_ref, sin_ref, ob_ref = refs
    elif mode == "k":
        u_ref, w_ref, cos_ref, sin_ref, of_ref, ob_ref = refs
    elif mode == "v":
        u_ref, w_ref, of_ref, ob_ref = refs
    else:
        u_ref, w_ref, of_ref = refs

    if mode == "gla":
        @pl.when(pl.program_id(1) == 0)
        def _():
            alow = jnp.dot(u_ref[...], wal_ref[...], preferred_element_type=F32)
            pre = jnp.dot(alow.astype(BF16), wa2_ref[...], preferred_element_type=F32) + ba_ref[...]
            la_ref[...] = (jnp.minimum(pre, 0.0) - jnp.log1p(jnp.exp(-jnp.abs(pre)))) * (1.0 / GLA_TAU)

    rr = min(EPILOGUE_ROWS, u_ref.shape[0])
    for r in range(u_ref.shape[0] // rr):
        rows = slice(r * rr, (r + 1) * rr)
        z = jnp.dot(u_ref[rows, :], w_ref[...], preferred_element_type=F32)
        if mode == "gla":
            z_ref[rows, :] = z
        elif mode == "q":
            ob_ref[rows, :] = (_rope(z, cos_ref[rows, :], sin_ref[rows, :]) * QUERY_SCALE).astype(BF16)
        elif mode == "k":
            zr = _rope(z, cos_ref[rows, :], sin_ref[rows, :])
            of_ref[rows, :] = zr
            ob_ref[rows, :] = zr.astype(BF16)
        elif mode == "v":
            of_ref[rows, :] = z
            ob_ref[rows, :] = z.astype(BF16)
        else:
            of_ref[rows, :] = 1.0 / (1.0 + jnp.exp(-z))


def _proj_call(mode, u, w, extra=(), tm_pref=1024, tn_pref=512):
    rows, d = u.shape
    n = w.shape[1]
    tm, tn = _tile(rows, tm_pref), _tile(n, tn_pref)
    row_blk = lambda i, j: (i, 0)
    fixed = lambda i, j: (0, 0)
    in_specs = [pl.BlockSpec((tm, d), row_blk), pl.BlockSpec((d, tn), lambda i, j: (0, j))]
    out_blk = pl.BlockSpec((tm, tn), lambda i, j: (i, j))
    if mode == "gla":
        wal, wa2, ba = extra
        in_specs += [pl.BlockSpec(wal.shape, fixed), pl.BlockSpec(wa2.shape, fixed), pl.BlockSpec(ba.shape, fixed)]
        out_shape = (jax.ShapeDtypeStruct((rows, n), F32), jax.ShapeDtypeStruct((rows, wa2.shape[1]), F32))
        out_specs = (out_blk, pl.BlockSpec((tm, wa2.shape[1]), row_blk))
    elif mode in ("q", "k"):
        in_specs += [pl.BlockSpec((tm, LANE), row_blk), pl.BlockSpec((tm, LANE), row_blk)]
        if mode == "q":
            out_shape, out_specs = jax.ShapeDtypeStruct((rows, n), BF16), out_blk
        else:
            out_shape = (jax.ShapeDtypeStruct((rows, n), F32), jax.ShapeDtypeStruct((rows, n), BF16))
            out_specs = (out_blk, out_blk)
    elif mode == "v":
        out_shape = (jax.ShapeDtypeStruct((rows, n), F32), jax.ShapeDtypeStruct((rows, n), BF16))
        out_specs = (out_blk, out_blk)
    else:
        out_shape, out_specs = jax.ShapeDtypeStruct((rows, n), F32), out_blk
    return pl.pallas_call(
        functools.partial(_proj_kernel, mode=mode),
        grid=(rows // tm, n // tn),
        in_specs=in_specs,
        out_specs=out_specs,
        out_shape=out_shape,
        compiler_params=_params(("parallel", "arbitrary")),
        name="proj_" + mode,
    )(u, w, *extra)


def _gla_kernel(q_ref, k_ref, v_ref, r_ref, la_ref, s0_ref, gn_ref, a_ref, sout_ref, st_ref, *, n_chunks):
    c = pl.program_id(2)

    @pl.when(c == 0)
    def _():
        st_ref[...] = s0_ref[...].T

    rsub = lax.broadcasted_iota(jnp.int32, (CHUNK, GLA_DK), 0) % SUB
    rr = lax.broadcasted_iota(jnp.int32, (CHUNK, N_SUB * CHUNK), 0)
    cc = lax.broadcasted_iota(jnp.int32, (CHUNK, N_SUB * CHUNK), 1)
    valid = jnp.where(cc // CHUNK == rr // SUB, cc % CHUNK, CHUNK) <= rr
    gn = gn_ref[...]
    zero = jnp.zeros((SUB, GLA_DK), F32)

    def prod(xs):
        out = xs[0]
        for x in xs[1:]:
            out = out * x
        return out

    def chunk(ci, carry):
        rows = pl.ds(pl.multiple_of(ci * CHUNK, CHUNK), CHUNK)
        q = q_ref[rows, :] * (GLA_DK ** -0.5)
        k = k_ref[rows, :]
        vb = v_ref[rows, :].astype(BF16)
        bl = la_ref[rows, :]
        for sh in (1, 2, 4, 8):
            bl = bl + jnp.where(rsub >= sh, pltpu.roll(bl, sh, 0), 0.0)
        et = [jnp.exp(bl[SUB * j + SUB - 1:SUB * (j + 1), :]) for j in range(N_SUB)]
        qt = q * jnp.exp(bl)
        kd = k * jnp.exp(-bl)
        qs = [qt[SUB * j:SUB * (j + 1)] for j in range(N_SUB)]
        kds = [kd[SUB * j:SUB * (j + 1)] for j in range(N_SUB)]
        khat = [kds[j] * et[j] for j in range(N_SUB)]

        blocks = []
        for i in range(N_SUB):
            for j in range(N_SUB):
                if j < i:
                    blocks.append(khat[j] if j + 1 == i else khat[j] * prod(et[j + 1:i]))
                else:
                    blocks.append(kds[j] if j == i else zero)
        kstack = jnp.concatenate(blocks, axis=0).astype(BF16)
        sc = lax.dot_general(qt.astype(BF16), kstack, _NT, preferred_element_type=F32)
        p = jnp.where(valid, sc, 0.0).astype(BF16)
        intra = jnp.dot(p, jnp.concatenate([vb] * N_SUB, axis=0), preferred_element_type=F32)

        qd = jnp.concatenate([qs[i] if i == 0 else qs[i] * prod(et[:i]) for i in range(N_SUB)], axis=0)
        st = st_ref[...]
        inter = lax.dot_general(qd.astype(BF16), st.astype(BF16), _NT, preferred_element_type=F32)
        o = inter + intra

        kdec = jnp.concatenate(
            [khat[j] if j == N_SUB - 1 else khat[j] * prod(et[j + 1:]) for j in range(N_SUB)], axis=0)
        st_ref[...] = st * prod(et) + lax.dot_general(vb, kdec.astype(BF16), _TN, preferred_element_type=F32)

        r = r_ref[rows, :]
        a_ref[rows, :] = (_rms(o) * gn * (r / (1.0 + jnp.exp(-r)))).astype(BF16)
        return carry

    lax.fori_loop(0, n_chunks, chunk, 0)

    @pl.when(c == pl.num_programs(2) - 1)
    def _():
        sout_ref[...] = st_ref[...].T


def _gla_call(zg, la, s0, gn, n_batch, tc_pref=512):
    rows = zg.shape[0]
    per_b = rows // n_batch
    tc = _tile(per_b, tc_pref)
    n_tc = per_b // tc
    s_stride = 0 if s0.shape[0] == 1 else 1
    row = lambda b, h, c: b * n_tc + c
    kq = GLA_HEADS * GLA_DK // GLA_DK
    kv = 2 * GLA_HEADS * GLA_DK // GLA_DV
    kr = kv + GLA_HEADS
    state_blk = (None, None, GLA_DK, GLA_DV)
    return pl.pallas_call(
        functools.partial(_gla_kernel, n_chunks=tc // CHUNK),
        grid=(n_batch, GLA_HEADS, n_tc),
        in_specs=[
            pl.BlockSpec((tc, GLA_DK), lambda b, h, c: (row(b, h, c), h)),
            pl.BlockSpec((tc, GLA_DK), lambda b, h, c: (row(b, h, c), kq + h)),
            pl.BlockSpec((tc, GLA_DV), lambda b, h, c: (row(b, h, c), kv + h)),
            pl.BlockSpec((tc, GLA_DV), lambda b, h, c: (row(b, h, c), kr + h)),
            pl.BlockSpec((tc, GLA_DK), lambda b, h, c: (row(b, h, c), h)),
            pl.BlockSpec(state_blk, lambda b, h, c: (b * s_stride, h, 0, 0)),
            pl.BlockSpec((1, GLA_DV), lambda b, h, c: (0, 0)),
        ],
        out_specs=(
            pl.BlockSpec((tc, GLA_DV), lambda b, h, c: (row(b, h, c), h)),
            pl.BlockSpec(state_blk, lambda b, h, c: (b, h, 0, 0)),
        ),
        out_shape=(
            jax.ShapeDtypeStruct((rows, GLA_HEADS * GLA_DV), BF16),
            jax.ShapeDtypeStruct((n_batch, GLA_HEADS, GLA_DK, GLA_DV), F32),
        ),
        scratch_shapes=[pltpu.VMEM((GLA_DV, GLA_DK), F32)],
        compiler_params=_params(("parallel", "parallel", "arbitrary")),
        name="gla",
    )(zg, zg, zg, zg, la, s0, gn)


def _lam(lq1, lk1, lq2, lk2, lam_init):
    return (jnp.exp(jnp.sum(lq1[...] * lk1[...], axis=1, keepdims=True))
            - jnp.exp(jnp.sum(lq2[...] * lk2[...], axis=1, keepdims=True)) + lam_init)


def _scores(q, k):
    s0 = lax.dot_general(q[:, :DIFF_DQK], k[:, :DIFF_DQK], _NT, preferred_element_type=F32)
    s1 = lax.dot_general(q[:, DIFF_DQK:], k[:, DIFF_DQK:], _NT, preferred_element_type=F32)
    return jnp.concatenate([s0, s1], axis=0)


def _diff_out(acc0, acc1, l0, l1, lam, dn, lam_init):
    o = acc0 * (1.0 / l0) - lam * (acc1 * (1.0 / l1))
    return (_rms(o) * dn * (1.0 - lam_init)).astype(BF16)


def _attn_prompt_kernel(q_ref, k_ref, v_ref, kp_ref, vp_ref, lq1, lk1, lq2, lk2, dn_ref, o_ref,
                        m_ref, l_ref, acc_ref, *, tq, tk, rs, lam_init):
    qi = pl.program_id(2)
    m_ref[...] = jnp.full(m_ref.shape, MASKED, F32)
    l_ref[...] = jnp.zeros(l_ref.shape, F32)
    acc_ref[...] = jnp.zeros(acc_ref.shape, F32)

    def tile_update(mp, r, kblk, vblk, mask=None):
        rows = slice(r * rs, (r + 1) * rs)
        cols = slice(mp * DIFF_DQK, (mp + 1) * DIFF_DQK)
        s = lax.dot_general(q_ref[rows, cols], kblk[:, cols], _NT, preferred_element_type=F32)
        if mask is not None:
            s = jnp.where(mask, s, MASKED)
        parts = [s[:, c * LANE:(c + 1) * LANE] for c in range(s.shape[1] // LANE)]
        smax = functools.reduce(jnp.maximum, parts)
        m_prev = m_ref[mp, rows, :]
        m_new = jnp.maximum(m_prev, jnp.max(smax, axis=1, keepdims=True))
        alpha = jnp.exp2(m_prev - m_new)
        ps = [jnp.exp2(x - m_new) for x in parts]
        l_ref[mp, rows, :] = alpha * l_ref[mp, rows, :] + functools.reduce(jnp.add, ps)
        pv = jnp.dot(jnp.concatenate(ps, axis=1).astype(BF16), vblk, preferred_element_type=F32)
        acc_ref[mp, rows, :] = jnp.concatenate([alpha] * (DIFF_DV // LANE), axis=1) * acc_ref[mp, rows, :] + pv
        m_ref[mp, rows, :] = m_new

    n_r = tq // rs
    pcol = lax.broadcasted_iota(jnp.int32, (rs, kp_ref.shape[0]), 1)
    pmask = (pcol >= META_LO) & (pcol < META_ROWS)
    for mp in range(2):
        for r in range(n_r):
            tile_update(mp, r, kp_ref[...], vp_ref[...], pmask)

    def full_block(kb, carry):
        krows = pl.ds(pl.multiple_of(kb * tk, tk), tk)
        kblk, vblk = k_ref[krows, :], v_ref[krows, :]
        for mp in range(2):
            for r in range(n_r):
                tile_update(mp, r, kblk, vblk)
        return carry

    lax.fori_loop(0, qi * (tq // tk), full_block, 0)

    cmask = (lax.broadcasted_iota(jnp.int32, (rs, rs), 1) // CHUNK
             <= lax.broadcasted_iota(jnp.int32, (rs, rs), 0) // CHUNK)
    for r in range(n_r):
        for d in range(r + 1):
            krows = pl.ds(pl.multiple_of(qi * tq + d * rs, rs), rs)
            kblk, vblk = k_ref[krows, :], v_ref[krows, :]
            for mp in range(2):
                tile_update(mp, r, kblk, vblk, cmask if d == r else None)

    lam = _lam(lq1, lk1, lq2, lk2, lam_init)
    l = jnp.sum(l_ref[...], axis=2, keepdims=True)
    o_ref[...] = _diff_out(acc_ref[0], acc_ref[1], l[0], l[1], lam, dn_ref[...], lam_init)


def _attn_prompt_call(q, kb, vb, kmeta, vmeta, lams, dn, n_batch, lam_init, tq_pref=2048, tk_pref=512, rs_pref=256):
    rows, width = q.shape
    t = rows // n_batch
    tq = _tile(t, tq_pref)
    tk = _tile(tq, tk_pref)
    rs = _tile(tq, rs_pref)
    nq = t // tq
    hw = 2 * DIFF_DQK
    fixed = lambda b, h, i: (0, 0)
    kv_spec = pl.BlockSpec((t, hw), lambda b, h, i: (b, h), pipeline_mode=pl.Buffered(1))
    meta_spec = pl.BlockSpec((kmeta.shape[0], hw), lambda b, h, i: (0, h))
    vec = pl.BlockSpec((1, DIFF_DQK), fixed)
    return pl.pallas_call(
        functools.partial(_attn_prompt_kernel, tq=tq, tk=tk, rs=rs, lam_init=lam_init),
        grid=(n_batch, DIFF_HEADS, nq),
        in_specs=[pl.BlockSpec((tq, hw), lambda b, h, i: (b * nq + i, h)), kv_spec, kv_spec, meta_spec, meta_spec,
                  vec, vec, vec, vec, pl.BlockSpec((1, DIFF_DV), fixed)],
        out_specs=pl.BlockSpec((tq, DIFF_DV), lambda b, h, i: (b * nq + i, h)),
        out_shape=jax.ShapeDtypeStruct((rows, width), BF16),
        scratch_shapes=[pltpu.VMEM((2, tq, LANE), F32), pltpu.VMEM((2, tq, LANE), F32), pltpu.VMEM((2, tq, DIFF_DV), F32)],
        compiler_params=_params(("parallel", "parallel", "arbitrary")),
        name="attn_prompt",
    )(q, kb, vb, kmeta, vmeta, *lams, dn)


def _attn_sample_kernel(q_ref, kc_ref, vlo_ref, vhi_ref, kn_ref, vn_ref, lq1, lk1, lq2, lk2, dn_ref, o_ref, *,
                        past, lam_init):
    lam = _lam(lq1, lk1, lq2, lk2, lam_init)
    dn = dn_ref[...]
    hw = 2 * DIFF_DQK
    for h in range(DIFF_HEADS):
        cols = slice(h * hw, (h + 1) * hw)
        q = q_ref[:, cols]
        tq = q.shape[0]
        kc = jnp.concatenate(
            [kc_ref[pl.ds(2 * h + mp, past, stride=2 * DIFF_HEADS), :] for mp in range(2)], axis=1).astype(BF16)
        vc = jnp.concatenate(
            [ref[pl.ds(h, past, stride=DIFF_HEADS), :] for ref in (vlo_ref, vhi_ref)], axis=1).astype(BF16)
        sc = _scores(q, kc)
        sn = _scores(q, kn_ref[:, cols])
        m = jnp.maximum(jnp.max(sc, axis=1, keepdims=True), jnp.max(sn, axis=1, keepdims=True))
        pc = jnp.exp2(sc - m)
        pn = jnp.exp2(sn - m)
        l = jnp.sum(pc, axis=1, keepdims=True) + jnp.sum(pn, axis=1, keepdims=True)
        acc = (jnp.dot(pc.astype(BF16), vc, preferred_element_type=F32)
               + jnp.dot(pn.astype(BF16), vn_ref[:, cols], preferred_element_type=F32))
        o_ref[:, cols] = _diff_out(acc[:tq], acc[tq:], l[:tq], l[tq:], lam, dn, lam_init)


def _attn_sample_call(q, kcache, vcache, knew, vnew, lams, dn, past, lam_init):
    rows, width = q.shape
    n_batch = kcache.shape[0]
    t = rows // n_batch
    fixed = lambda b: (0, 0)
    new_spec = pl.BlockSpec((t, width), lambda b: (b, 0))
    vec = pl.BlockSpec((1, DIFF_DQK), fixed)
    return pl.pallas_call(
        functools.partial(_attn_sample_kernel, past=past, lam_init=lam_init),
        grid=(n_batch,),
        in_specs=[new_spec, pl.BlockSpec((None,) + kcache.shape[1:], lambda b: (b, 0, 0)),
                  pl.BlockSpec((None, vcache.shape[1], LANE), lambda b: (b, 0, 0)),
                  pl.BlockSpec((None, vcache.shape[1], LANE), lambda b: (b, 0, 1)), new_spec, new_spec,
                  vec, vec, vec, vec, pl.BlockSpec((1, DIFF_DV), fixed)],
        out_specs=new_spec,
        out_shape=jax.ShapeDtypeStruct((rows, width), BF16),
        compiler_params=_params(("parallel",)),
        name="attn_sample",
    )(q, kcache, vcache, vcache, knew, vnew, *lams, dn)


def _mlp2_kernel(*refs, mode):
    if mode == "merge":
        a_ref, d_ref, ga_ref, gb_ref, wa_ref, wb_ref, w2_ref, res_ref, gpost_ref, o_ref = refs
    else:
        res_ref, gpre_ref, w1_ref, w2_ref, gpost_ref, o_ref, u_ref = refs
    j = pl.program_id(1)

    if mode == "ffn":
        @pl.when(j == 0)
        def _():
            u_ref[...] = (_rms(res_ref[...]) * gpre_ref[...]).astype(BF16)

    @pl.when(j == 0)
    def _():
        o_ref[...] = jnp.zeros(o_ref.shape, F32)

    rr = min(MLP_ROWS, o_ref.shape[0])
    for r in range(o_ref.shape[0] // rr):
        rows = slice(r * rr, (r + 1) * rr)
        if mode == "merge":
            mid = (ga_ref[rows, :] * jnp.dot(a_ref[rows, :], wa_ref[...], preferred_element_type=F32)
                   + gb_ref[rows, :] * jnp.dot(d_ref[rows, :], wb_ref[...], preferred_element_type=F32))
        else:
            mid = jnp.square(jnp.maximum(jnp.dot(u_ref[rows, :], w1_ref[...], preferred_element_type=F32), 0.0))
        o_ref[rows, :] += jnp.dot(mid.astype(BF16), w2_ref[...], preferred_element_type=F32)

    @pl.when(j == pl.num_programs(1) - 1)
    def _():
        o_ref[...] = res_ref[...] + _rms(o_ref[...]) * gpost_ref[...]


def _merge_call(a, d, gates, wa, wb, wo, res, gpost, tm_pref=512, tn_pref=512):
    rows, dm = res.shape
    k = a.shape[1]
    tm, tn = _tile(rows, tm_pref), _tile(dm, tn_pref)
    nj = dm // tn
    row_blk = lambda i, j: (i, 0)
    return pl.pallas_call(
        functools.partial(_mlp2_kernel, mode="merge"),
        grid=(rows // tm, nj),
        in_specs=[
            pl.BlockSpec((tm, k), row_blk), pl.BlockSpec((tm, k), row_blk),
            pl.BlockSpec((tm, tn), lambda i, j: (i, j)), pl.BlockSpec((tm, tn), lambda i, j: (i, nj + j)),
            pl.BlockSpec((k, tn), lambda i, j: (0, j)), pl.BlockSpec((k, tn), lambda i, j: (0, j)),
            pl.BlockSpec((tn, dm), lambda i, j: (j, 0)),
            pl.BlockSpec((tm, dm), row_blk), pl.BlockSpec((1, dm), lambda i, j: (0, 0)),
        ],
        out_specs=pl.BlockSpec((tm, dm), row_blk),
        out_shape=jax.ShapeDtypeStruct((rows, dm), F32),
        compiler_params=_params(("parallel", "arbitrary")),
        name="merge",
    )(a, d, gates, gates, wa, wb, wo, res, gpost)


def _ffn_call(h, gpre, w1, w2, gpost, tm_pref=512, tf_pref=1024):
    rows, dm = h.shape
    dff = w1.shape[1]
    tm, tf = _tile(rows, tm_pref), _tile(dff, tf_pref)
    row_blk = lambda i, j: (i, 0)
    vec = pl.BlockSpec((1, dm), lambda i, j: (0, 0))
    return pl.pallas_call(
        functools.partial(_mlp2_kernel, mode="ffn"),
        grid=(rows // tm, dff // tf),
        in_specs=[pl.BlockSpec((tm, dm), row_blk), vec, pl.BlockSpec((dm, tf), lambda i, j: (0, j)),
                  pl.BlockSpec((tf, dm), lambda i, j: (j, 0)), vec],
        out_specs=pl.BlockSpec((tm, dm), row_blk),
        out_shape=jax.ShapeDtypeStruct((rows, dm), F32),
        scratch_shapes=[pltpu.VMEM((tm, dm), BF16)],
        compiler_params=_params(("parallel", "arbitrary")),
        name="ffn",
    )(h, gpre, w1, w2, gpost)


def _rope_tables(pos):
    inv_freq = jnp.power(ROPE_THETA, -jnp.arange(0, ROPE_DIM, 2, dtype=F32) / ROPE_DIM)
    ang = pos[:, None] * inv_freq[None, :]
    cos, sin = jnp.cos(ang), jnp.sin(ang)
    n = pos.shape[0]
    pad = LANE - ROPE_DIM
    return (jnp.concatenate([cos, cos, jnp.ones((n, pad), F32)], axis=1),
            jnp.concatenate([-sin, sin, jnp.zeros((n, pad), F32)], axis=1))


def kernel(x_prompt, x_sample, cache_k, cache_v, state_gla, meta, norm_mix_pre, w_in, w_gla_a2, b_gla_a, gla_norm, diff_lq1, diff_lk1, diff_lq2, diff_lk2, diff_norm, w_br_gla, w_br_diff, w_o, norm_mix_post, norm_ffn_pre, w_ff1, w_ff2, norm_ffn_post):
    n_b, seq, dm = x_prompt.shape
    n_db, dec_seq, _ = x_sample.shape
    past = cache_k.shape[2]
    assert w_in.shape[0] == 1, "single-layer step only"
    assert dec_seq == CHUNK and seq % CHUNK == 0 and meta.shape[0] == N_META
    lam_init = 0.8 - 0.6 * math.exp(-0.3 * 0)

    sizes = (GLA_HEADS * GLA_DK, GLA_HEADS * GLA_DK, GLA_HEADS * GLA_DV, GATE_RANK, GLA_HEADS * GLA_DV,
             2 * DIFF_HEADS * DIFF_DQK, 2 * DIFF_HEADS * DIFF_DQK, DIFF_HEADS * DIFF_DV, dm, dm)
    offs = [0]
    for s in sizes:
        offs.append(offs[-1] + s)
    wi = w_in[0]
    col = lambda i: wi[:, offs[i]:offs[i + 1]]
    w_gla = jnp.concatenate([col(0), col(1), col(2), col(4)], axis=1).astype(BF16)
    w_alow = jnp.pad(col(3), ((0, 0), (0, LANE - GATE_RANK))).astype(BF16)
    w_a2 = jnp.pad(w_gla_a2[0], ((0, LANE - GATE_RANK), (0, 0))).astype(BF16)
    w_q, w_k, w_v = col(5).astype(BF16), col(6).astype(BF16), col(7).astype(BF16)
    w_gate = jnp.concatenate([col(8), col(9)], axis=1).astype(BF16)
    b_a = b_gla_a
    g_pre = norm_mix_pre
    wa, wb, wo = w_br_gla[0].astype(BF16), w_br_diff[0].astype(BF16), w_o[0].astype(BF16)
    w1, w2 = w_ff1[0].astype(BF16), w_ff2[0].astype(BF16)
    lams = (diff_lq1, diff_lk1, diff_lq2, diff_lk2)

    xp = x_prompt.reshape(n_b * seq, dm)
    xs = x_sample.reshape(n_db * dec_seq, dm)
    xm = jnp.concatenate([jnp.zeros((META_LO, dm), x_prompt.dtype), meta.astype(x_prompt.dtype)], axis=0)
    cos_p, sin_p = _rope_tables(jnp.tile(jnp.arange(seq, dtype=F32) + N_META, n_b))
    cos_s, sin_s = _rope_tables(jnp.tile(jnp.arange(dec_seq, dtype=F32) + past, n_db))
    cos_m, sin_m = _rope_tables(jnp.arange(META_ROWS, dtype=F32) - META_LO)

    def project(x, cos, sin, with_q):
        u = _norm_call(x, g_pre)
        zg, la = _proj_call("gla", u, w_gla, (w_alow, w_a2, b_a))
        kf, kb = _proj_call("k", u, w_k, (cos, sin))
        vf, vb = _proj_call("v", u, w_v)
        if not with_q:
            return zg, la, kf, kb, vf, vb
        q = _proj_call("q", u, w_q, (cos, sin))
        gates = _proj_call("gate", u, w_gate)
        return zg, la, kf, kb, vf, vb, q, gates

    zg_m, la_m, kf_m, kb_m, vf_m, vb_m = project(xm, cos_m, sin_m, False)
    zero_state = jnp.zeros((1, GLA_HEADS, GLA_DK, GLA_DV), F32)
    _, s_meta = _gla_call(zg_m, la_m, zero_state, gla_norm, 1)

    zg, la, kf_p, kb_p, vf_p, vb_p, q_p, gates_p = project(xp, cos_p, sin_p, True)
    a_p, s_p = _gla_call(zg, la, s_meta, gla_norm, n_b)
    key_pad = ((0, LANE - META_ROWS), (0, 0))
    d_p = _attn_prompt_call(q_p, kb_p, vb_p, jnp.pad(kb_m, key_pad), jnp.pad(vb_m, key_pad), lams, diff_norm,
                            n_b, lam_init)
    h_p = _merge_call(a_p, d_p, gates_p, wa, wb, wo, xp, norm_mix_post)
    y_p = _ffn_call(h_p, norm_ffn_pre, w1, w2, norm_ffn_post)

    zg, la, kf_s, kb_s, vf_s, vb_s, q_s, gates_s = project(xs, cos_s, sin_s, True)
    a_s, s_s = _gla_call(zg, la, state_gla[0].astype(F32), gla_norm, n_db)
    kc = cache_k[0].reshape(n_db, past * DIFF_HEADS * 2, DIFF_DQK)
    vc = cache_v[0].reshape(n_db, past * DIFF_HEADS, DIFF_DV)
    d_s = _attn_sample_call(q_s, kc, vc, kb_s, vb_s, lams, diff_norm, past, lam_init)
    h_s = _merge_call(a_s, d_s, gates_s, wa, wb, wo, xs, norm_mix_post)
    y_s = _ffn_call(h_s, norm_ffn_pre, w1, w2, norm_ffn_post)

    def with_meta(f_meta, f_real):
        m = jnp.broadcast_to(f_meta[None, META_LO:], (n_b, N_META, f_meta.shape[1]))
        return jnp.concatenate([m, f_real.reshape(n_b, seq, -1)], axis=1)

    k_shape = (DIFF_HEADS, 2, DIFF_DQK)
    v_shape = (DIFF_HEADS, DIFF_DV)
    return (y_p.reshape(n_b, seq, dm),
            y_s.reshape(n_db, dec_seq, dm),
            with_meta(kf_m, kf_p).reshape((1, n_b, N_META + seq) + k_shape),
            with_meta(vf_m, vf_p).reshape((1, n_b, N_META + seq) + v_shape),
            s_p[None],
            kf_s.reshape((1, n_db, dec_seq) + k_shape),
            vf_s.reshape((1, n_db, dec_seq) + v_shape),
            s_s[None])
```

```python
import functools
import math

import jax
import jax.numpy as jnp
from jax import lax
from jax.experimental import pallas as pl
from jax.experimental.pallas import tpu as pltpu

F32 = jnp.float32
BF16 = jnp.bfloat16

EPS = 1e-6
CHUNK = 64
SUB = 16
N_SUB = CHUNK // SUB
N_META = 16
META_ROWS = CHUNK
META_LO = META_ROWS - N_META
GLA_HEADS = 4
GLA_DK = 256
GLA_DV = 512
GATE_RANK = 16
GLA_TAU = 16.0
DIFF_HEADS = 8
DIFF_DQK = 128
DIFF_DV = 256
ROPE_DIM = 32
ROPE_THETA = 500000.0
LANE = 128
QUERY_SCALE = DIFF_DQK ** -0.5 * math.log2(math.e)
EPILOGUE_ROWS = 128
MLP_ROWS = 256
MASKED = -1e30
VMEM_LIMIT = 56 * 1024 * 1024

_NT = (((1,), (1,)), ((), ()))
_TN = (((0,), (0,)), ((), ()))


def _tile(n, pref):
    return pref if n % pref == 0 else n


def _params(sem):
    return pltpu.CompilerParams(dimension_semantics=sem, vmem_limit_bytes=VMEM_LIMIT)


def _rms(x):
    return x * lax.rsqrt(jnp.mean(x * x, axis=-1, keepdims=True) + EPS)


def _rope(z, cos, sin):
    half = ROPE_DIM // 2
    lane = lax.broadcasted_iota(jnp.int32, (z.shape[0], LANE), 1)
    out = []
    for g in range(z.shape[1] // LANE):
        seg = z[:, g * LANE:(g + 1) * LANE]
        partner = jnp.where(lane < half, pltpu.roll(seg, LANE - half, 1), pltpu.roll(seg, half, 1))
        out.append(seg * cos + partner * sin)
    return jnp.concatenate(out, axis=1)


def _norm_kernel(x_ref, g_ref, u_ref):
    u_ref[...] = (_rms(x_ref[...]) * g_ref[...]).astype(BF16)


def _norm_call(x, g, tm_pref=512):
    rows, d = x.shape
    tm = _tile(rows, tm_pref)
    return pl.pallas_call(
        _norm_kernel,
        grid=(rows // tm,),
        in_specs=[pl.BlockSpec((tm, d), lambda i: (i, 0)), pl.BlockSpec((1, d), lambda i: (0, 0))],
        out_specs=pl.BlockSpec((tm, d), lambda i: (i, 0)),
        out_shape=jax.ShapeDtypeStruct((rows, d), BF16),
        compiler_params=_params(("parallel",)),
        name="norm",
    )(x, g)


def _proj_kernel(*refs, mode):
    if mode == "gla":
        u_ref, w_ref, wal_ref, wa2_ref, ba_ref, z_ref, la_ref = refs
    elif mode == "q":
        u_ref, w_ref, cos_ref, sin_ref, ob_ref = refs
    elif mode == "k":
        u_ref, w_ref, cos_ref, sin_ref, of_ref, ob_ref = refs
    elif mode == "v":
        u_ref, w_ref, of_ref, ob_ref = refs
    else:
        u_ref, w_ref, of_ref = refs

    if mode == "gla":
        @pl.when(pl.program_id(1) == 0)
        def _():
            alow = jnp.dot(u_ref[...], wal_ref[...], preferred_element_type=F32)
            pre = jnp.dot(alow.astype(BF16), wa2_ref[...], preferred_element_type=F32) + ba_ref[...]
            la_ref[...] = (jnp.minimum(pre, 0.0) - jnp.log1p(jnp.exp(-jnp.abs(pre)))) * (1.0 / GLA_TAU)

    rr = min(EPILOGUE_ROWS, u_ref.shape[0])
    for r in range(u_ref.shape[0] // rr):
        rows = slice(r * rr, (r + 1) * rr)
        z = jnp.dot(u_ref[rows, :], w_ref[...], preferred_element_type=F32)
        if mode == "gla":
            z_ref[rows, :] = z
        elif mode == "q":
            ob_ref[rows, :] = (_rope(z, cos_ref[rows, :], sin_ref[rows, :]) * QUERY_SCALE).astype(BF16)
        elif mode == "k":
            zr = _rope(z, cos_ref[rows, :], sin_ref[rows, :])
            ob_ref[rows, :] = zr.astype(BF16)
            n_g = zr.shape[1] // LANE
            for g in range(n_g):
                of_ref[pl.ds(r * rr * n_g + g, rr, stride=n_g), :] = zr[:, g * LANE:(g + 1) * LANE]
        elif mode == "v":
            of_ref[rows, :] = z
            ob_ref[rows, :] = z.astype(BF16)
        else:
            of_ref[rows, :] = 1.0 / (1.0 + jnp.exp(-z))


def _proj_call(mode, u, w, extra=(), tm_pref=2048, tn_pref=512, lead_rows=0, n_lead=1):
    rows, d = u.shape
    n = w.shape[1]
    tm, tn = _tile(rows // n_lead, tm_pref), _tile(n, tn_pref)
    tiles_g = rows // n_lead // tm
    row_blk = lambda i, j: (i, 0)
    fixed = lambda i, j: (0, 0)

    def lead_row(i, width):
        return pl.multiple_of(((i // tiles_g) * (tiles_g * tm + lead_rows) + lead_rows + (i % tiles_g) * tm) * width, 8)

    in_specs = [pl.BlockSpec((tm, d), row_blk), pl.BlockSpec((d, tn), lambda i, j: (0, j))]
    out_blk = pl.BlockSpec((tm, tn), lambda i, j: (i, j))
    if mode == "gla":
        wal, wa2, ba = extra
        in_specs += [pl.BlockSpec(wal.shape, fixed), pl.BlockSpec(wa2.shape, fixed), pl.BlockSpec(ba.shape, fixed)]
        out_shape = (jax.ShapeDtypeStruct((rows, n), F32), jax.ShapeDtypeStruct((rows, wa2.shape[1]), F32))
        out_specs = (out_blk, pl.BlockSpec((tm, wa2.shape[1]), row_blk))
    elif mode in ("q", "k"):
        in_specs += [pl.BlockSpec((tm, LANE), row_blk), pl.BlockSpec((tm, LANE), row_blk)]
        if mode == "q":
            out_shape, out_specs = jax.ShapeDtypeStruct((rows, n), BF16), out_blk
        else:
            assert tn == n, "native-order key rows need every (head, map) group of a frame in one block"
            n_g = n // LANE
            out_shape = (jax.ShapeDtypeStruct(((rows + n_lead * lead_rows) * n_g, LANE), F32),
                         jax.ShapeDtypeStruct((rows, n), BF16))
            out_specs = (pl.BlockSpec((pl.Element(tm * n_g), pl.Element(LANE)), lambda i, j: (lead_row(i, n_g), 0)),
                         out_blk)
    elif mode == "v":
        out_shape = (jax.ShapeDtypeStruct((rows + n_lead * lead_rows, n), F32), jax.ShapeDtypeStruct((rows, n), BF16))
        out_specs = (pl.BlockSpec((pl.Element(tm), pl.Element(tn)),
                                  lambda i, j: (lead_row(i, 1), pl.multiple_of(j * tn, LANE))), out_blk)
    else:
        out_shape, out_specs = jax.ShapeDtypeStruct((rows, n), F32), out_blk
    return pl.pallas_call(
        functools.partial(_proj_kernel, mode=mode),
        grid=(rows // tm, n // tn),
        in_specs=in_specs,
        out_specs=out_specs,
        out_shape=out_shape,
        compiler_params=_params(("parallel", "arbitrary")),
        name="proj_" + mode,
    )(u, w, *extra)


def _gla_kernel(zg_ref, la_ref, s0_ref, gn_ref, a_ref, sout_ref, st_ref, *, n_chunks):
    c = pl.program_id(1)

    @pl.when(c == 0)
    def _():
        for h in range(GLA_HEADS):
            st_ref[h] = s0_ref[h].T

    rsub = lax.broadcasted_iota(jnp.int32, (CHUNK, GLA_DK), 0) % SUB
    rr = lax.broadcasted_iota(jnp.int32, (CHUNK, N_SUB * CHUNK), 0)
    cc = lax.broadcasted_iota(jnp.int32, (CHUNK, N_SUB * CHUNK), 1)
    valid = jnp.where(cc // CHUNK == rr // SUB, cc % CHUNK, CHUNK) <= rr
    gn = gn_ref[...]
    zero = jnp.zeros((SUB, GLA_DK), F32)
    k_off = GLA_HEADS * GLA_DK
    v_off = 2 * GLA_HEADS * GLA_DK
    r_off = v_off + GLA_HEADS * GLA_DV

    def prod(xs):
        out = xs[0]
        for x in xs[1:]:
            out = out * x
        return out

    def head_chunk(h, rows):
        q = zg_ref[rows, h * GLA_DK:(h + 1) * GLA_DK] * (GLA_DK ** -0.5)
        k = zg_ref[rows, k_off + h * GLA_DK:k_off + (h + 1) * GLA_DK]
        vb = zg_ref[rows, v_off + h * GLA_DV:v_off + (h + 1) * GLA_DV].astype(BF16)
        bl = la_ref[rows, h * GLA_DK:(h + 1) * GLA_DK]
        for sh in (1, 2, 4, 8):
            bl = bl + jnp.where(rsub >= sh, pltpu.roll(bl, sh, 0), 0.0)
        et = [jnp.exp(bl[SUB * j + SUB - 1:SUB * (j + 1), :]) for j in range(N_SUB)]
        qt = q * jnp.exp(bl)
        kd = k * jnp.exp(-bl)
        qs = [qt[SUB * j:SUB * (j + 1)] for j in range(N_SUB)]
        kds = [kd[SUB * j:SUB * (j + 1)] for j in range(N_SUB)]
        khat = [kds[j] * et[j] for j in range(N_SUB)]

        blocks = []
        for i in range(N_SUB):
            for j in range(N_SUB):
                if j < i:
                    blocks.append(khat[j] if j + 1 == i else khat[j] * prod(et[j + 1:i]))
                else:
                    blocks.append(kds[j] if j == i else zero)
        kstack = jnp.concatenate(blocks, axis=0).astype(BF16)
        sc = lax.dot_general(qt.astype(BF16), kstack, _NT, preferred_element_type=F32)
        p = jnp.where(valid, sc, 0.0).astype(BF16)
        intra = jnp.dot(p, jnp.concatenate([vb] * N_SUB, axis=0), preferred_element_type=F32)

        qd = jnp.concatenate([qs[i] if i == 0 else qs[i] * prod(et[:i]) for i in range(N_SUB)], axis=0)
        st = st_ref[h]
        inter = lax.dot_general(qd.astype(BF16), st.astype(BF16), _NT, preferred_element_type=F32)
        o = inter + intra

        kdec = jnp.concatenate(
            [khat[j] if j == N_SUB - 1 else khat[j] * prod(et[j + 1:]) for j in range(N_SUB)], axis=0)
        st_ref[h] = st * prod(et) + lax.dot_general(vb, kdec.astype(BF16), _TN, preferred_element_type=F32)

        r = zg_ref[rows, r_off + h * GLA_DV:r_off + (h + 1) * GLA_DV]
        a_ref[rows, h * GLA_DV:(h + 1) * GLA_DV] = (_rms(o) * gn * (r / (1.0 + jnp.exp(-r)))).astype(BF16)

    def chunk(ci, carry):
        rows = pl.ds(pl.multiple_of(ci * CHUNK, CHUNK), CHUNK)
        for h in range(GLA_HEADS):
            head_chunk(h, rows)
        return carry

    lax.fori_loop(0, n_chunks, chunk, 0)

    @pl.when(c == pl.num_programs(1) - 1)
    def _():
        for h in range(GLA_HEADS):
            sout_ref[h] = st_ref[h].T


def _gla_call(zg, la, s0, gn, n_batch, tc_pref=256):
    rows = zg.shape[0]
    per_b = rows // n_batch
    tc = _tile(per_b, tc_pref)
    n_tc = per_b // tc
    s_stride = 0 if s0.shape[0] == 1 else 1
    row_blk = lambda b, c: (b * n_tc + c, 0)
    state_blk = (None, GLA_HEADS, GLA_DK, GLA_DV)
    return pl.pallas_call(
        functools.partial(_gla_kernel, n_chunks=tc // CHUNK),
        grid=(n_batch, n_tc),
        in_specs=[
            pl.BlockSpec((tc, zg.shape[1]), row_blk),
            pl.BlockSpec((tc, la.shape[1]), row_blk),
            pl.BlockSpec(state_blk, lambda b, c: (b * s_stride, 0, 0, 0)),
            pl.BlockSpec((1, GLA_DV), lambda b, c: (0, 0)),
        ],
        out_specs=(
            pl.BlockSpec((tc, GLA_HEADS * GLA_DV), row_blk),
            pl.BlockSpec(state_blk, lambda b, c: (b, 0, 0, 0)),
        ),
        out_shape=(
            jax.ShapeDtypeStruct((rows, GLA_HEADS * GLA_DV), BF16),
            jax.ShapeDtypeStruct((n_batch, GLA_HEADS, GLA_DK, GLA_DV), F32),
        ),
        scratch_shapes=[pltpu.VMEM((GLA_HEADS, GLA_DV, GLA_DK), F32)],
        compiler_params=_params(("parallel", "arbitrary")),
        name="gla",
    )(zg, la, s0, gn)


def _lam(lq1, lk1, lq2, lk2, lam_init):
    return (jnp.exp(jnp.sum(lq1[...] * lk1[...], axis=1, keepdims=True))
            - jnp.exp(jnp.sum(lq2[...] * lk2[...], axis=1, keepdims=True)) + lam_init)


def _scores(q, k):
    s0 = lax.dot_general(q[:, :DIFF_DQK], k[:, :DIFF_DQK], _NT, preferred_element_type=F32)
    s1 = lax.dot_general(q[:, DIFF_DQK:], k[:, DIFF_DQK:], _NT, preferred_element_type=F32)
    return jnp.concatenate([s0, s1], axis=0)


def _diff_out(acc0, acc1, l0, l1, lam, dn, lam_init):
    o = acc0 * (1.0 / l0) - lam * (acc1 * (1.0 / l1))
    return (_rms(o) * dn * (1.0 - lam_init)).astype(BF16)


def _attn_prompt_kernel(q_ref, k_ref, v_ref, kp_ref, vp_ref, lq1, lk1, lq2, lk2, dn_ref, o_ref,
                        m_ref, l_ref, acc_ref, *, tq, tk, rs, lam_init):
    qi = pl.program_id(2)
    m_ref[...] = jnp.full(m_ref.shape, MASKED, F32)
    l_ref[...] = jnp.zeros(l_ref.shape, F32)
    acc_ref[...] = jnp.zeros(acc_ref.shape, F32)

    def tile_update(mp, r, kblk, vblk, mask=None):
        rows = slice(r * rs, (r + 1) * rs)
        cols = slice(mp * DIFF_DQK, (mp + 1) * DIFF_DQK)
        s = lax.dot_general(q_ref[rows, cols], kblk[:, cols], _NT, preferred_element_type=F32)
        if mask is not None:
            s = jnp.where(mask, s, MASKED)
        parts = [s[:, c * LANE:(c + 1) * LANE] for c in range(s.shape[1] // LANE)]
        smax = functools.reduce(jnp.maximum, parts)
        m_prev = m_ref[mp, rows, :]
        m_new = jnp.maximum(m_prev, jnp.max(smax, axis=1, keepdims=True))
        alpha = jnp.exp2(m_prev - m_new)
        ps = [jnp.exp2(x - m_new) for x in parts]
        l_ref[mp, rows, :] = alpha * l_ref[mp, rows, :] + functools.reduce(jnp.add, ps)
        pv = jnp.dot(jnp.concatenate(ps, axis=1).astype(BF16), vblk, preferred_element_type=F32)
        acc_ref[mp, rows, :] = jnp.concatenate([alpha] * (DIFF_DV // LANE), axis=1) * acc_ref[mp, rows, :] + pv
        m_ref[mp, rows, :] = m_new

    n_r = tq // rs
    pcol = lax.broadcasted_iota(jnp.int32, (rs, kp_ref.shape[0]), 1)
    pmask = (pcol >= META_LO) & (pcol < META_ROWS)
    for mp in range(2):
        for r in range(n_r):
            tile_update(mp, r, kp_ref[...], vp_ref[...], pmask)

    def full_block(kb, carry):
        krows = pl.ds(pl.multiple_of(kb * tk, tk), tk)
        kblk, vblk = k_ref[krows, :], v_ref[krows, :]
        for mp in range(2):
            for r in range(n_r):
                tile_update(mp, r, kblk, vblk)
        return carry

    lax.fori_loop(0, qi * (tq // tk), full_block, 0)

    cmask = (lax.broadcasted_iota(jnp.int32, (rs, rs), 1) // CHUNK
             <= lax.broadcasted_iota(jnp.int32, (rs, rs), 0) // CHUNK)
    for r in range(n_r):
        for d in range(r + 1):
            krows = pl.ds(pl.multiple_of(qi * tq + d * rs, rs), rs)
            kblk, vblk = k_ref[krows, :], v_ref[krows, :]
            for mp in range(2):
                tile_update(mp, r, kblk, vblk, cmask if d == r else None)

    lam = _lam(lq1, lk1, lq2, lk2, lam_init)
    l = jnp.sum(l_ref[...], axis=2, keepdims=True)
    o_ref[...] = _diff_out(acc_ref[0], acc_ref[1], l[0], l[1], lam, dn_ref[...], lam_init)


def _attn_prompt_call(q, kb, vb, kmeta, vmeta, lams, dn, n_batch, lam_init, tq_pref=2048, tk_pref=512, rs_pref=256):
    rows, width = q.shape
    t = rows // n_batch
    tq = _tile(t, tq_pref)
    tk = _tile(tq, tk_pref)
    rs = _tile(tq, rs_pref)
    nq = t // tq
    hw = 2 * DIFF_DQK
    fixed = lambda b, h, i: (0, 0)
    kv_spec = pl.BlockSpec((t, hw), lambda b, h, i: (b, h), pipeline_mode=pl.Buffered(1))
    meta_spec = pl.BlockSpec((kmeta.shape[0], hw), lambda b, h, i: (0, h))
    vec = pl.BlockSpec((1, DIFF_DQK), fixed)
    return pl.pallas_call(
        functools.partial(_attn_prompt_kernel, tq=tq, tk=tk, rs=rs, lam_init=lam_init),
        grid=(n_batch, DIFF_HEADS, nq),
        in_specs=[pl.BlockSpec((tq, hw), lambda b, h, i: (b * nq + i, h)), kv_spec, kv_spec, meta_spec, meta_spec,
                  vec, vec, vec, vec, pl.BlockSpec((1, DIFF_DV), fixed)],
        out_specs=pl.BlockSpec((tq, DIFF_DV), lambda b, h, i: (b * nq + i, h)),
        out_shape=jax.ShapeDtypeStruct((rows, width), BF16),
        scratch_shapes=[pltpu.VMEM((2, tq, LANE), F32), pltpu.VMEM((2, tq, LANE), F32), pltpu.VMEM((2, tq, DIFF_DV), F32)],
        compiler_params=_params(("parallel", "parallel", "arbitrary")),
        name="attn_prompt",
    )(q, kb, vb, kmeta, vmeta, *lams, dn)


def _attn_sample_kernel(q_ref, kc_ref, vlo_ref, vhi_ref, kn_ref, vn_ref, lq1, lk1, lq2, lk2, dn_ref, o_ref, *,
                        past, lam_init):
    lam = _lam(lq1, lk1, lq2, lk2, lam_init)
    dn = dn_ref[...]
    hw = 2 * DIFF_DQK
    for h in range(DIFF_HEADS):
        cols = slice(h * hw, (h + 1) * hw)
        q = q_ref[:, cols]
        tq = q.shape[0]
        kc = jnp.concatenate(
            [kc_ref[pl.ds(2 * h + mp, past, stride=2 * DIFF_HEADS), :] for mp in range(2)], axis=1).astype(BF16)
        vc = jnp.concatenate(
            [ref[pl.ds(h, past, stride=DIFF_HEADS), :] for ref in (vlo_ref, vhi_ref)], axis=1).astype(BF16)
        sc = _scores(q, kc)
        sn = _scores(q, kn_ref[:, cols])
        m = jnp.maximum(jnp.max(sc, axis=1, keepdims=True), jnp.max(sn, axis=1, keepdims=True))
        pc = jnp.exp2(sc - m)
        pn = jnp.exp2(sn - m)
        l = jnp.sum(pc, axis=1, keepdims=True) + jnp.sum(pn, axis=1, keepdims=True)
        acc = (jnp.dot(pc.astype(BF16), vc, preferred_element_type=F32)
               + jnp.dot(pn.astype(BF16), vn_ref[:, cols], preferred_element_type=F32))
        o_ref[:, cols] = _diff_out(acc[:tq], acc[tq:], l[:tq], l[tq:], lam, dn, lam_init)


def _attn_sample_call(q, kcache, vcache, knew, vnew, lams, dn, past, lam_init):
    rows, width = q.shape
    n_batch = kcache.shape[0]
    t = rows // n_batch
    fixed = lambda b: (0, 0)
    new_spec = pl.BlockSpec((t, width), lambda b: (b, 0))
    vec = pl.BlockSpec((1, DIFF_DQK), fixed)
    return pl.pallas_call(
        functools.partial(_attn_sample_kernel, past=past, lam_init=lam_init),
        grid=(n_batch,),
        in_specs=[new_spec, pl.BlockSpec((None,) + kcache.shape[1:], lambda b: (b, 0, 0)),
                  pl.BlockSpec((None, vcache.shape[1], LANE), lambda b: (b, 0, 0)),
                  pl.BlockSpec((None, vcache.shape[1], LANE), lambda b: (b, 0, 1)), new_spec, new_spec,
                  vec, vec, vec, vec, pl.BlockSpec((1, DIFF_DV), fixed)],
        out_specs=new_spec,
        out_shape=jax.ShapeDtypeStruct((rows, width), BF16),
        compiler_params=_params(("parallel",)),
        name="attn_sample",
    )(q, kcache, vcache, vcache, knew, vnew, *lams, dn)


def _mlp2_kernel(*refs, mode):
    if mode == "merge":
        a_ref, d_ref, ga_ref, gb_ref, wa_ref, wb_ref, w2_ref, res_ref, gpost_ref, o_ref = refs
    else:
        res_ref, gpre_ref, w1_ref, w2_ref, gpost_ref, o_ref, u_ref = refs
    j = pl.program_id(1)

    if mode == "ffn":
        @pl.when(j == 0)
        def _():
            u_ref[...] = (_rms(res_ref[...]) * gpre_ref[...]).astype(BF16)

    @pl.when(j == 0)
    def _():
        o_ref[...] = jnp.zeros(o_ref.shape, F32)

    rr = min(MLP_ROWS, o_ref.shape[0])
    for r in range(o_ref.shape[0] // rr):
        rows = slice(r * rr, (r + 1) * rr)
        if mode == "merge":
            mid = (ga_ref[rows, :] * jnp.dot(a_ref[rows, :], wa_ref[...], preferred_element_type=F32)
                   + gb_ref[rows, :] * jnp.dot(d_ref[rows, :], wb_ref[...], preferred_element_type=F32))
        else:
            mid = jnp.square(jnp.maximum(jnp.dot(u_ref[rows, :], w1_ref[...], preferred_element_type=F32), 0.0))
        o_ref[rows, :] += jnp.dot(mid.astype(BF16), w2_ref[...], preferred_element_type=F32)

    @pl.when(j == pl.num_programs(1) - 1)
    def _():
        o_ref[...] = res_ref[...] + _rms(o_ref[...]) * gpost_ref[...]


def _merge_call(a, d, gates, wa, wb, wo, res, gpost, tm_pref=512, tn_pref=512):
    rows, dm = res.shape
    k = a.shape[1]
    tm, tn = _tile(rows, tm_pref), _tile(dm, tn_pref)
    nj = dm // tn
    row_blk = lambda i, j: (i, 0)
    return pl.pallas_call(
        functools.partial(_mlp2_kernel, mode="merge"),
        grid=(rows // tm, nj),
        in_specs=[
            pl.BlockSpec((tm, k), row_blk), pl.BlockSpec((tm, k), row_blk),
            pl.BlockSpec((tm, tn), lambda i, j: (i, j)), pl.BlockSpec((tm, tn), lambda i, j: (i, nj + j)),
            pl.BlockSpec((k, tn), lambda i, j: (0, j)), pl.BlockSpec((k, tn), lambda i, j: (0, j)),
            pl.BlockSpec((tn, dm), lambda i, j: (j, 0)),
            pl.BlockSpec((tm, dm), row_blk), pl.BlockSpec((1, dm), lambda i, j: (0, 0)),
        ],
        out_specs=pl.BlockSpec((tm, dm), row_blk),
        out_shape=jax.ShapeDtypeStruct((rows, dm), F32),
        compiler_params=_params(("parallel", "arbitrary")),
        name="merge",
    )(a, d, gates, gates, wa, wb, wo, res, gpost)


def _ffn_call(h, gpre, w1, w2, gpost, tm_pref=512, tf_pref=1024):
    rows, dm = h.shape
    dff = w1.shape[1]
    tm, tf = _tile(rows, tm_pref), _tile(dff, tf_pref)
    row_blk = lambda i, j: (i, 0)
    vec = pl.BlockSpec((1, dm), lambda i, j: (0, 0))
    return pl.pallas_call(
        functools.partial(_mlp2_kernel, mode="ffn"),
        grid=(rows // tm, dff // tf),
        in_specs=[pl.BlockSpec((tm, dm), row_blk), vec, pl.BlockSpec((dm, tf), lambda i, j: (0, j)),
                  pl.BlockSpec((tf, dm), lambda i, j: (j, 0)), vec],
        out_specs=pl.BlockSpec((tm, dm), row_blk),
        out_shape=jax.ShapeDtypeStruct((rows, dm), F32),
        scratch_shapes=[pltpu.VMEM((tm, dm), BF16)],
        compiler_params=_params(("parallel", "arbitrary")),
        name="ffn",
    )(h, gpre, w1, w2, gpost)


def _rope_tables(pos):
    inv_freq = jnp.power(ROPE_THETA, -jnp.arange(0, ROPE_DIM, 2, dtype=F32) / ROPE_DIM)
    ang = pos[:, None] * inv_freq[None, :]
    cos, sin = jnp.cos(ang), jnp.sin(ang)
    n = pos.shape[0]
    pad = LANE - ROPE_DIM
    return (jnp.concatenate([cos, cos, jnp.ones((n, pad), F32)], axis=1),
            jnp.concatenate([-sin, sin, jnp.zeros((n, pad), F32)], axis=1))


def kernel(x_prompt, x_sample, cache_k, cache_v, state_gla, meta, norm_mix_pre, w_in, w_gla_a2, b_gla_a, gla_norm, diff_lq1, diff_lk1, diff_lq2, diff_lk2, diff_norm, w_br_gla, w_br_diff, w_o, norm_mix_post, norm_ffn_pre, w_ff1, w_ff2, norm_ffn_post):
    n_b, seq, dm = x_prompt.shape
    n_db, dec_seq, _ = x_sample.shape
    past = cache_k.shape[2]
    assert w_in.shape[0] == 1, "single-layer step only"
    assert dec_seq == CHUNK and seq % CHUNK == 0 and meta.shape[0] == N_META
    lam_init = 0.8 - 0.6 * math.exp(-0.3 * 0)

    sizes = (GLA_HEADS * GLA_DK, GLA_HEADS * GLA_DK, GLA_HEADS * GLA_DV, GATE_RANK, GLA_HEADS * GLA_DV,
             2 * DIFF_HEADS * DIFF_DQK, 2 * DIFF_HEADS * DIFF_DQK, DIFF_HEADS * DIFF_DV, dm, dm)
    offs = [0]
    for s in sizes:
        offs.append(offs[-1] + s)
    wi = w_in[0]
    col = lambda i: wi[:, offs[i]:offs[i + 1]]
    w_gla = jnp.concatenate([col(0), col(1), col(2), col(4)], axis=1).astype(BF16)
    w_alow = jnp.pad(col(3), ((0, 0), (0, LANE - GATE_RANK))).astype(BF16)
    w_a2 = jnp.pad(w_gla_a2[0], ((0, LANE - GATE_RANK), (0, 0))).astype(BF16)
    w_q, w_k, w_v = col(5).astype(BF16), col(6).astype(BF16), col(7).astype(BF16)
    w_gate = jnp.concatenate([col(8), col(9)], axis=1).astype(BF16)
    b_a = b_gla_a
    g_pre = norm_mix_pre
    wa, wb, wo = w_br_gla[0].astype(BF16), w_br_diff[0].astype(BF16), w_o[0].astype(BF16)
    w1, w2 = w_ff1[0].astype(BF16), w_ff2[0].astype(BF16)
    lams = (diff_lq1, diff_lk1, diff_lq2, diff_lk2)

    xp = x_prompt.reshape(n_b * seq, dm)
    xs = x_sample.reshape(n_db * dec_seq, dm)
    xm = jnp.concatenate([jnp.zeros((META_LO, dm), x_prompt.dtype), meta.astype(x_prompt.dtype)], axis=0)
    cos_p, sin_p = _rope_tables(jnp.tile(jnp.arange(seq, dtype=F32) + N_META, n_b))
    cos_s, sin_s = _rope_tables(jnp.tile(jnp.arange(dec_seq, dtype=F32) + past, n_db))
    cos_m, sin_m = _rope_tables(jnp.arange(META_ROWS, dtype=F32) - META_LO)

    def project(x, cos, sin, with_q, lead_rows=0, n_lead=1):
        u = _norm_call(x, g_pre)
        zg, la = _proj_call("gla", u, w_gla, (w_alow, w_a2, b_a))
        kf, kb = _proj_call("k", u, w_k, (cos, sin), tm_pref=512, tn_pref=w_k.shape[1],
                            lead_rows=lead_rows, n_lead=n_lead)
        vf, vb = _proj_call("v", u, w_v, lead_rows=lead_rows, n_lead=n_lead)
        if not with_q:
            return zg, la, kf, kb, vf, vb
        q = _proj_call("q", u, w_q, (cos, sin))
        gates = _proj_call("gate", u, w_gate)
        return zg, la, kf, kb, vf, vb, q, gates

    zg_m, la_m, kf_m, kb_m, vf_m, vb_m = project(xm, cos_m, sin_m, False)
    zero_state = jnp.zeros((1, GLA_HEADS, GLA_DK, GLA_DV), F32)
    _, s_meta = _gla_call(zg_m, la_m, zero_state, gla_norm, 1)

    zg, la, kf_p, kb_p, vf_p, vb_p, q_p, gates_p = project(xp, cos_p, sin_p, True, N_META, n_b)
    a_p, s_p = _gla_call(zg, la, s_meta, gla_norm, n_b)
    key_pad = ((0, LANE - META_ROWS), (0, 0))
    d_p = _attn_prompt_call(q_p, kb_p, vb_p, jnp.pad(kb_m, key_pad), jnp.pad(vb_m, key_pad), lams, diff_norm,
                            n_b, lam_init)
    h_p = _merge_call(a_p, d_p, gates_p, wa, wb, wo, xp, norm_mix_post)
    y_p = _ffn_call(h_p, norm_ffn_pre, w1, w2, norm_ffn_post)

    zg, la, kf_s, kb_s, vf_s, vb_s, q_s, gates_s = project(xs, cos_s, sin_s, True)
    a_s, s_s = _gla_call(zg, la, state_gla[0].astype(F32), gla_norm, n_db)
    kc = cache_k[0].reshape(n_db, past * DIFF_HEADS * 2, DIFF_DQK)
    vc = cache_v[0].reshape(n_db, past * DIFF_HEADS, DIFF_DV)
    d_s = _attn_sample_call(q_s, kc, vc, kb_s, vb_s, lams, diff_norm, past, lam_init)
    h_s = _merge_call(a_s, d_s, gates_s, wa, wb, wo, xs, norm_mix_post)
    y_s = _ffn_call(h_s, norm_ffn_pre, w1, w2, norm_ffn_post)

    def with_meta(f_meta, f_real):
        per_frame = f_meta.shape[0] // META_ROWS
        m = f_meta[None, META_LO * per_frame:]
        return f_real.reshape(n_b, -1, f_real.shape[1]).at[:, :N_META * per_frame].set(
            jnp.broadcast_to(m, (n_b,) + m.shape[1:]))

    k_shape = (DIFF_HEADS, 2, DIFF_DQK)
    v_shape = (DIFF_HEADS, DIFF_DV)
    return (y_p.reshape(n_b, seq, dm),
            y_s.reshape(n_db, dec_seq, dm),
            with_meta(kf_m, kf_p).reshape((1, n_b, N_META + seq) + k_shape),
            with_meta(vf_m, vf_p).reshape((1, n_b, N_META + seq) + v_shape),
            s_p[None],
            kf_s.reshape((1, n_db, dec_seq) + k_shape),
            vf_s.reshape((1, n_db, dec_seq) + v_shape),
            s_s[None])
```

```python
import functools
import math

import jax
import jax.numpy as jnp
from jax import lax
from jax.experimental import pallas as pl
from jax.experimental.pallas import tpu as pltpu

F32 = jnp.float32
BF16 = jnp.bfloat16

EPS = 1e-6
CHUNK = 64
SUB = 16
N_SUB = CHUNK // SUB
N_META = 16
META_ROWS = CHUNK
META_LO = META_ROWS - N_META
GLA_HEADS = 4
GLA_DK = 256
GLA_DV = 512
GATE_RANK = 16
GLA_TAU = 16.0
DIFF_HEADS = 8
DIFF_DQK = 128
DIFF_DV = 256
ROPE_DIM = 32
ROPE_THETA = 500000.0
LANE = 128
QUERY_SCALE = DIFF_DQK ** -0.5 * math.log2(math.e)
EPILOGUE_ROWS = 128
MLP_ROWS = 256
MASKED = -1e30
VMEM_LIMIT = 56 * 1024 * 1024

_NT = (((1,), (1,)), ((), ()))
_TN = (((0,), (0,)), ((), ()))


def _tile(n, pref):
    return pref if n % pref == 0 else n


def _params(sem):
    return pltpu.CompilerParams(dimension_semantics=sem, vmem_limit_bytes=VMEM_LIMIT)


def _rms(x):
    return x * lax.rsqrt(jnp.mean(x * x, axis=-1, keepdims=True) + EPS)


def _rope(z, cos, sin):
    half = ROPE_DIM // 2
    lane = lax.broadcasted_iota(jnp.int32, (z.shape[0], LANE), 1)
    out = []
    for g in range(z.shape[1] // LANE):
        seg = z[:, g * LANE:(g + 1) * LANE]
        partner = jnp.where(lane < half, pltpu.roll(seg, LANE - half, 1), pltpu.roll(seg, half, 1))
        out.append(seg * cos + partner * sin)
    return jnp.concatenate(out, axis=1)


def _norm_kernel(x_ref, g_ref, u_ref):
    u_ref[...] = (_rms(x_ref[...]) * g_ref[...]).astype(BF16)


def _norm_call(x, g, tm_pref=512):
    rows, d = x.shape
    tm = _tile(rows, tm_pref)
    return pl.pallas_call(
        _norm_kernel,
        grid=(rows // tm,),
        in_specs=[pl.BlockSpec((tm, d), lambda i: (i, 0)), pl.BlockSpec((1, d), lambda i: (0, 0))],
        out_specs=pl.BlockSpec((tm, d), lambda i: (i, 0)),
        out_shape=jax.ShapeDtypeStruct((rows, d), BF16),
        compiler_params=_params(("parallel",)),
        name="norm",
    )(x, g)


def _proj_kernel(*refs, mode):
    if mode == "gla":
        u_ref, w_ref, wal_ref, wa2_ref, ba_ref, z_ref, la_ref = refs
    elif mode == "q":
        u_ref, w_ref, cos_ref, sin_ref, ob_ref = refs
    elif mode == "k":
        u_ref, w_ref, cos_ref, sin_ref, of_ref, ob_ref = refs
    elif mode == "v":
        u_ref, w_ref, of_ref, ob_ref = refs
    else:
        u_ref, w_ref, ob_ref = refs

    if mode == "gla":
        @pl.when(pl.program_id(1) == 0)
        def _():
            alow = jnp.dot(u_ref[...], wal_ref[...], preferred_element_type=F32)
            pre = jnp.dot(alow.astype(BF16), wa2_ref[...], preferred_element_type=F32) + ba_ref[...]
            la_ref[...] = (jnp.minimum(pre, 0.0) - jnp.log1p(jnp.exp(-jnp.abs(pre)))) * (1.0 / GLA_TAU)

    rr = min(EPILOGUE_ROWS, u_ref.shape[0])
    for r in range(u_ref.shape[0] // rr):
        rows = slice(r * rr, (r + 1) * rr)
        z = jnp.dot(u_ref[rows, :], w_ref[...], preferred_element_type=F32)
        if mode == "gla":
            z_ref[rows, :] = z
        elif mode == "q":
            ob_ref[rows, :] = (_rope(z, cos_ref[rows, :], sin_ref[rows, :]) * QUERY_SCALE).astype(BF16)
        elif mode == "k":
            zr = _rope(z, cos_ref[rows, :], sin_ref[rows, :])
            ob_ref[rows, :] = zr.astype(BF16)
            n_g = zr.shape[1] // LANE
            for g in range(n_g):
                of_ref[pl.ds(r * rr * n_g + g, rr, stride=n_g), :] = zr[:, g * LANE:(g + 1) * LANE]
        elif mode == "v":
            of_ref[rows, :] = z
            ob_ref[rows, :] = z.astype(BF16)
        elif mode == "gate":
            ob_ref[rows, :] = (1.0 / (1.0 + jnp.exp(-z))).astype(BF16)
        else:
            ob_ref[rows, :] = z.astype(BF16)


def _proj_call(mode, u, w, extra=(), tm_pref=2048, tn_pref=512, lead_rows=0, n_lead=1):
    rows, d = u.shape
    n = w.shape[1]
    tm, tn = _tile(rows // n_lead, tm_pref), _tile(n, tn_pref)
    tiles_g = rows // n_lead // tm
    row_blk = lambda i, j: (i, 0)
    fixed = lambda i, j: (0, 0)

    def lead_row(i, width):
        return pl.multiple_of(((i // tiles_g) * (tiles_g * tm + lead_rows) + lead_rows + (i % tiles_g) * tm) * width, 8)

    in_specs = [pl.BlockSpec((tm, d), row_blk), pl.BlockSpec((d, tn), lambda i, j: (0, j))]
    out_blk = pl.BlockSpec((tm, tn), lambda i, j: (i, j))
    if mode == "gla":
        wal, wa2, ba = extra
        in_specs += [pl.BlockSpec(wal.shape, fixed), pl.BlockSpec(wa2.shape, fixed), pl.BlockSpec(ba.shape, fixed)]
        out_shape = (jax.ShapeDtypeStruct((rows, n), F32), jax.ShapeDtypeStruct((rows, wa2.shape[1]), F32))
        out_specs = (out_blk, pl.BlockSpec((tm, wa2.shape[1]), row_blk))
    elif mode in ("q", "k"):
        table_blk = lambda i, j: (i % (extra[0].shape[0] // tm), 0)
        in_specs += [pl.BlockSpec((tm, LANE), table_blk), pl.BlockSpec((tm, LANE), table_blk)]
        if mode == "q":
            out_shape, out_specs = jax.ShapeDtypeStruct((rows, n), BF16), out_blk
        else:
            assert tn == n, "native-order key rows need every (head, map) group of a frame in one block"
            n_g = n // LANE
            out_shape = (jax.ShapeDtypeStruct(((rows + n_lead * lead_rows) * n_g, LANE), F32),
                         jax.ShapeDtypeStruct((rows, n), BF16))
            out_specs = (pl.BlockSpec((pl.Element(tm * n_g), pl.Element(LANE)), lambda i, j: (lead_row(i, n_g), 0)),
                         out_blk)
    elif mode == "v":
        out_shape = (jax.ShapeDtypeStruct((rows + n_lead * lead_rows, n), F32), jax.ShapeDtypeStruct((rows, n), BF16))
        out_specs = (pl.BlockSpec((pl.Element(tm), pl.Element(tn)),
                                  lambda i, j: (lead_row(i, 1), pl.multiple_of(j * tn, LANE))), out_blk)
    else:
        out_shape, out_specs = jax.ShapeDtypeStruct((rows, n), BF16), out_blk
    return pl.pallas_call(
        functools.partial(_proj_kernel, mode=mode),
        grid=(rows // tm, n // tn),
        in_specs=in_specs,
        out_specs=out_specs,
        out_shape=out_shape,
        compiler_params=_params(("parallel", "arbitrary")),
        name="proj_" + mode,
    )(u, w, *extra)


def _gla_kernel(zqk_ref, zvr_ref, la_ref, s0_ref, gn_ref, a_ref, sout_ref, st_ref, *, n_chunks):
    c = pl.program_id(1)

    @pl.when(c == 0)
    def _():
        for h in range(GLA_HEADS):
            st_ref[h] = s0_ref[h].T

    rsub = lax.broadcasted_iota(jnp.int32, (CHUNK, GLA_DK), 0) % SUB
    rr = lax.broadcasted_iota(jnp.int32, (CHUNK, N_SUB * CHUNK), 0)
    cc = lax.broadcasted_iota(jnp.int32, (CHUNK, N_SUB * CHUNK), 1)
    valid = jnp.where(cc // CHUNK == rr // SUB, cc % CHUNK, CHUNK) <= rr
    gn = gn_ref[...]
    zero = jnp.zeros((SUB, GLA_DK), F32)
    k_off = GLA_HEADS * GLA_DK
    r_off = GLA_HEADS * GLA_DV

    def prod(xs):
        out = xs[0]
        for x in xs[1:]:
            out = out * x
        return out

    def head_chunk(h, rows):
        q = zqk_ref[rows, h * GLA_DK:(h + 1) * GLA_DK] * (GLA_DK ** -0.5)
        k = zqk_ref[rows, k_off + h * GLA_DK:k_off + (h + 1) * GLA_DK]
        vb = zvr_ref[rows, h * GLA_DV:(h + 1) * GLA_DV]
        bl = la_ref[rows, h * GLA_DK:(h + 1) * GLA_DK]
        for sh in (1, 2, 4, 8):
            bl = bl + jnp.where(rsub >= sh, pltpu.roll(bl, sh, 0), 0.0)
        et = [jnp.exp(bl[SUB * j + SUB - 1:SUB * (j + 1), :]) for j in range(N_SUB)]
        qt = q * jnp.exp(bl)
        kd = k * jnp.exp(-bl)
        qs = [qt[SUB * j:SUB * (j + 1)] for j in range(N_SUB)]
        kds = [kd[SUB * j:SUB * (j + 1)] for j in range(N_SUB)]
        khat = [kds[j] * et[j] for j in range(N_SUB)]

        blocks = []
        for i in range(N_SUB):
            for j in range(N_SUB):
                if j < i:
                    blocks.append(khat[j] if j + 1 == i else khat[j] * prod(et[j + 1:i]))
                else:
                    blocks.append(kds[j] if j == i else zero)
        kstack = jnp.concatenate(blocks, axis=0).astype(BF16)
        sc = lax.dot_general(qt.astype(BF16), kstack, _NT, preferred_element_type=F32)
        p = jnp.where(valid, sc, 0.0).astype(BF16)
        intra = jnp.dot(p, jnp.concatenate([vb] * N_SUB, axis=0), preferred_element_type=F32)

        qd = jnp.concatenate([qs[i] if i == 0 else qs[i] * prod(et[:i]) for i in range(N_SUB)], axis=0)
        st = st_ref[h]
        inter = lax.dot_general(qd.astype(BF16), st.astype(BF16), _NT, preferred_element_type=F32)
        o = inter + intra

        kdec = jnp.concatenate(
            [khat[j] if j == N_SUB - 1 else khat[j] * prod(et[j + 1:]) for j in range(N_SUB)], axis=0)
        st_ref[h] = st * prod(et) + lax.dot_general(vb, kdec.astype(BF16), _TN, preferred_element_type=F32)

        r = zvr_ref[rows, r_off + h * GLA_DV:r_off + (h + 1) * GLA_DV].astype(F32)
        a_ref[rows, h * GLA_DV:(h + 1) * GLA_DV] = (_rms(o) * gn * (r / (1.0 + jnp.exp(-r)))).astype(BF16)

    def chunk(ci, carry):
        rows = pl.ds(pl.multiple_of(ci * CHUNK, CHUNK), CHUNK)
        for h in range(GLA_HEADS):
            head_chunk(h, rows)
        return carry

    lax.fori_loop(0, n_chunks, chunk, 0)

    @pl.when(c == pl.num_programs(1) - 1)
    def _():
        for h in range(GLA_HEADS):
            sout_ref[h] = st_ref[h].T


def _gla_call(zqk, zvr, la, s0, gn, n_batch, tc_pref=256):
    rows = zqk.shape[0]
    per_b = rows // n_batch
    tc = _tile(per_b, tc_pref)
    n_tc = per_b // tc
    s_stride = 0 if s0.shape[0] == 1 else 1
    row_blk = lambda b, c: (b * n_tc + c, 0)
    state_blk = (None, GLA_HEADS, GLA_DK, GLA_DV)
    return pl.pallas_call(
        functools.partial(_gla_kernel, n_chunks=tc // CHUNK),
        grid=(n_batch, n_tc),
        in_specs=[
            pl.BlockSpec((tc, zqk.shape[1]), row_blk),
            pl.BlockSpec((tc, zvr.shape[1]), row_blk),
            pl.BlockSpec((tc, la.shape[1]), row_blk),
            pl.BlockSpec(state_blk, lambda b, c: (b * s_stride, 0, 0, 0)),
            pl.BlockSpec((1, GLA_DV), lambda b, c: (0, 0)),
        ],
        out_specs=(
            pl.BlockSpec((tc, GLA_HEADS * GLA_DV), row_blk),
            pl.BlockSpec(state_blk, lambda b, c: (b, 0, 0, 0)),
        ),
        out_shape=(
            jax.ShapeDtypeStruct((rows, GLA_HEADS * GLA_DV), BF16),
            jax.ShapeDtypeStruct((n_batch, GLA_HEADS, GLA_DK, GLA_DV), F32),
        ),
        scratch_shapes=[pltpu.VMEM((GLA_HEADS, GLA_DV, GLA_DK), F32)],
        compiler_params=_params(("parallel", "arbitrary")),
        name="gla",
    )(zqk, zvr, la, s0, gn)


def _lam(lq1, lk1, lq2, lk2, lam_init):
    return (jnp.exp(jnp.sum(lq1[...] * lk1[...], axis=1, keepdims=True))
            - jnp.exp(jnp.sum(lq2[...] * lk2[...], axis=1, keepdims=True)) + lam_init)


def _scores(q, k):
    s0 = lax.dot_general(q[:, :DIFF_DQK], k[:, :DIFF_DQK], _NT, preferred_element_type=F32)
    s1 = lax.dot_general(q[:, DIFF_DQK:], k[:, DIFF_DQK:], _NT, preferred_element_type=F32)
    return jnp.concatenate([s0, s1], axis=0)


def _diff_out(acc0, acc1, l0, l1, lam, dn, lam_init):
    o = acc0 * (1.0 / l0) - lam * (acc1 * (1.0 / l1))
    return (_rms(o) * dn * (1.0 - lam_init)).astype(BF16)


def _attn_prompt_kernel(q_ref, k_ref, v_ref, kp_ref, vp_ref, lq1, lk1, lq2, lk2, dn_ref, o_ref,
                        m_ref, l_ref, acc_ref, *, tq, tk, rs, lam_init):
    qi = pl.program_id(2)
    m_ref[...] = jnp.full(m_ref.shape, MASKED, F32)
    l_ref[...] = jnp.zeros(l_ref.shape, F32)
    acc_ref[...] = jnp.zeros(acc_ref.shape, F32)

    def tile_update(mp, r, kblk, vblk, mask=None):
        rows = slice(r * rs, (r + 1) * rs)
        cols = slice(mp * DIFF_DQK, (mp + 1) * DIFF_DQK)
        s = lax.dot_general(q_ref[rows, cols], kblk[:, cols], _NT, preferred_element_type=F32)
        if mask is not None:
            s = jnp.where(mask, s, MASKED)
        parts = [s[:, c * LANE:(c + 1) * LANE] for c in range(s.shape[1] // LANE)]
        smax = functools.reduce(jnp.maximum, parts)
        m_prev = m_ref[mp, rows, :]
        m_new = jnp.maximum(m_prev, jnp.max(smax, axis=1, keepdims=True))
        alpha = jnp.exp2(m_prev - m_new)
        ps = [jnp.exp2(x - m_new) for x in parts]
        l_ref[mp, rows, :] = alpha * l_ref[mp, rows, :] + functools.reduce(jnp.add, ps)
        pv = jnp.dot(jnp.concatenate(ps, axis=1).astype(BF16), vblk, preferred_element_type=F32)
        acc_ref[mp, rows, :] = jnp.concatenate([alpha] * (DIFF_DV // LANE), axis=1) * acc_ref[mp, rows, :] + pv
        m_ref[mp, rows, :] = m_new

    n_r = tq // rs
    pcol = lax.broadcasted_iota(jnp.int32, (rs, kp_ref.shape[0]), 1)
    pmask = (pcol >= META_LO) & (pcol < META_ROWS)
    for mp in range(2):
        for r in range(n_r):
            tile_update(mp, r, kp_ref[...], vp_ref[...], pmask)

    def full_block(kb, carry):
        krows = pl.ds(pl.multiple_of(kb * tk, tk), tk)
        kblk, vblk = k_ref[krows, :], v_ref[krows, :]
        for mp in range(2):
            for r in range(n_r):
                tile_update(mp, r, kblk, vblk)
        return carry

    lax.fori_loop(0, qi * (tq // tk), full_block, 0)

    cmask = (lax.broadcasted_iota(jnp.int32, (rs, rs), 1) // CHUNK
             <= lax.broadcasted_iota(jnp.int32, (rs, rs), 0) // CHUNK)
    for r in range(n_r):
        for d in range(r + 1):
            krows = pl.ds(pl.multiple_of(qi * tq + d * rs, rs), rs)
            kblk, vblk = k_ref[krows, :], v_ref[krows, :]
            for mp in range(2):
                tile_update(mp, r, kblk, vblk, cmask if d == r else None)

    lam = _lam(lq1, lk1, lq2, lk2, lam_init)
    l = jnp.sum(l_ref[...], axis=2, keepdims=True)
    o_ref[...] = _diff_out(acc_ref[0], acc_ref[1], l[0], l[1], lam, dn_ref[...], lam_init)


def _attn_prompt_call(q, kb, vb, kmeta, vmeta, lams, dn, n_batch, lam_init, tq_pref=2048, tk_pref=512, rs_pref=256):
    rows, width = q.shape
    t = rows // n_batch
    tq = _tile(t, tq_pref)
    tk = _tile(tq, tk_pref)
    rs = _tile(tq, rs_pref)
    nq = t // tq
    hw = 2 * DIFF_DQK
    fixed = lambda b, h, i: (0, 0)
    kv_spec = pl.BlockSpec((t, hw), lambda b, h, i: (b, h), pipeline_mode=pl.Buffered(1))
    meta_spec = pl.BlockSpec((kmeta.shape[0], hw), lambda b, h, i: (0, h))
    vec = pl.BlockSpec((1, DIFF_DQK), fixed)
    return pl.pallas_call(
        functools.partial(_attn_prompt_kernel, tq=tq, tk=tk, rs=rs, lam_init=lam_init),
        grid=(n_batch, DIFF_HEADS, nq),
        in_specs=[pl.BlockSpec((tq, hw), lambda b, h, i: (b * nq + i, h)), kv_spec, kv_spec, meta_spec, meta_spec,
                  vec, vec, vec, vec, pl.BlockSpec((1, DIFF_DV), fixed)],
        out_specs=pl.BlockSpec((tq, DIFF_DV), lambda b, h, i: (b * nq + i, h)),
        out_shape=jax.ShapeDtypeStruct((rows, width), BF16),
        scratch_shapes=[pltpu.VMEM((2, tq, LANE), F32), pltpu.VMEM((2, tq, LANE), F32), pltpu.VMEM((2, tq, DIFF_DV), F32)],
        compiler_params=_params(("parallel", "parallel", "arbitrary")),
        name="attn_prompt",
    )(q, kb, vb, kmeta, vmeta, *lams, dn)


def _attn_sample_kernel(q_ref, kc_ref, vlo_ref, vhi_ref, kn_ref, vn_ref, lq1, lk1, lq2, lk2, dn_ref, o_ref, *,
                        past, lam_init):
    lam = _lam(lq1, lk1, lq2, lk2, lam_init)
    dn = dn_ref[...]
    hw = 2 * DIFF_DQK
    for h in range(DIFF_HEADS):
        cols = slice(h * hw, (h + 1) * hw)
        q = q_ref[:, cols]
        tq = q.shape[0]
        kc = jnp.concatenate(
            [kc_ref[pl.ds(2 * h + mp, past, stride=2 * DIFF_HEADS), :] for mp in range(2)], axis=1).astype(BF16)
        vc = jnp.concatenate(
            [ref[pl.ds(h, past, stride=DIFF_HEADS), :] for ref in (vlo_ref, vhi_ref)], axis=1).astype(BF16)
        sc = _scores(q, kc)
        sn = _scores(q, kn_ref[:, cols])
        m = jnp.maximum(jnp.max(sc, axis=1, keepdims=True), jnp.max(sn, axis=1, keepdims=True))
        pc = jnp.exp2(sc - m)
        pn = jnp.exp2(sn - m)
        l = jnp.sum(pc, axis=1, keepdims=True) + jnp.sum(pn, axis=1, keepdims=True)
        acc = (jnp.dot(pc.astype(BF16), vc, preferred_element_type=F32)
               + jnp.dot(pn.astype(BF16), vn_ref[:, cols], preferred_element_type=F32))
        o_ref[:, cols] = _diff_out(acc[:tq], acc[tq:], l[:tq], l[tq:], lam, dn, lam_init)


def _attn_sample_call(q, kcache, vcache, knew, vnew, lams, dn, past, lam_init):
    rows, width = q.shape
    n_batch = kcache.shape[0]
    t = rows // n_batch
    fixed = lambda b: (0, 0)
    new_spec = pl.BlockSpec((t, width), lambda b: (b, 0))
    vec = pl.BlockSpec((1, DIFF_DQK), fixed)
    return pl.pallas_call(
        functools.partial(_attn_sample_kernel, past=past, lam_init=lam_init),
        grid=(n_batch,),
        in_specs=[new_spec, pl.BlockSpec((None,) + kcache.shape[1:], lambda b: (b, 0, 0)),
                  pl.BlockSpec((None, vcache.shape[1], LANE), lambda b: (b, 0, 0)),
                  pl.BlockSpec((None, vcache.shape[1], LANE), lambda b: (b, 0, 1)), new_spec, new_spec,
                  vec, vec, vec, vec, pl.BlockSpec((1, DIFF_DV), fixed)],
        out_specs=new_spec,
        out_shape=jax.ShapeDtypeStruct((rows, width), BF16),
        compiler_params=_params(("parallel",)),
        name="attn_sample",
    )(q, kcache, vcache, vcache, knew, vnew, *lams, dn)


def _mlp2_kernel(*refs, mode):
    if mode == "merge":
        a_ref, d_ref, ga_ref, gb_ref, wa_ref, wb_ref, w2_ref, res_ref, gpost_ref, gnext_ref, o_ref, unext_ref = refs
    else:
        res_ref, u_ref, w1_ref, w2_ref, gpost_ref, o_ref = refs
    j = pl.program_id(1)

    @pl.when(j == 0)
    def _():
        o_ref[...] = jnp.zeros(o_ref.shape, F32)

    rr = min(MLP_ROWS, o_ref.shape[0])
    for r in range(o_ref.shape[0] // rr):
        rows = slice(r * rr, (r + 1) * rr)
        if mode == "merge":
            mid = (ga_ref[rows, :] * jnp.dot(a_ref[rows, :], wa_ref[...], preferred_element_type=F32)
                   + gb_ref[rows, :] * jnp.dot(d_ref[rows, :], wb_ref[...], preferred_element_type=F32))
        else:
            mid = jnp.square(jnp.maximum(jnp.dot(u_ref[rows, :], w1_ref[...], preferred_element_type=F32), 0.0))
        o_ref[rows, :] += jnp.dot(mid.astype(BF16), w2_ref[...], preferred_element_type=F32)

    @pl.when(j == pl.num_programs(1) - 1)
    def _():
        out = res_ref[...] + _rms(o_ref[...]) * gpost_ref[...]
        o_ref[...] = out
        if mode == "merge":
            unext_ref[...] = (_rms(out) * gnext_ref[...]).astype(BF16)


def _merge_call(a, d, gates, wa, wb, wo, res, gpost, gnext, tm_pref=512, tn_pref=512):
    rows, dm = res.shape
    k = a.shape[1]
    tm, tn = _tile(rows, tm_pref), _tile(dm, tn_pref)
    nj = dm // tn
    row_blk = lambda i, j: (i, 0)
    return pl.pallas_call(
        functools.partial(_mlp2_kernel, mode="merge"),
        grid=(rows // tm, nj),
        in_specs=[
            pl.BlockSpec((tm, k), row_blk), pl.BlockSpec((tm, k), row_blk),
            pl.BlockSpec((tm, tn), lambda i, j: (i, j)), pl.BlockSpec((tm, tn), lambda i, j: (i, nj + j)),
            pl.BlockSpec((k, tn), lambda i, j: (0, j)), pl.BlockSpec((k, tn), lambda i, j: (0, j)),
            pl.BlockSpec((tn, dm), lambda i, j: (j, 0)),
            pl.BlockSpec((tm, dm), row_blk), pl.BlockSpec((1, dm), lambda i, j: (0, 0)),
            pl.BlockSpec((1, dm), lambda i, j: (0, 0)),
        ],
        out_specs=(pl.BlockSpec((tm, dm), row_blk), pl.BlockSpec((tm, dm), row_blk)),
        out_shape=(jax.ShapeDtypeStruct((rows, dm), F32), jax.ShapeDtypeStruct((rows, dm), BF16)),
        compiler_params=_params(("parallel", "arbitrary")),
        name="merge",
    )(a, d, gates, gates, wa, wb, wo, res, gpost, gnext)


def _ffn_call(h, u, w1, w2, gpost, tm_pref=512, tf_pref=1024):
    rows, dm = h.shape
    dff = w1.shape[1]
    tm, tf = _tile(rows, tm_pref), _tile(dff, tf_pref)
    row_blk = lambda i, j: (i, 0)
    vec = pl.BlockSpec((1, dm), lambda i, j: (0, 0))
    return pl.pallas_call(
        functools.partial(_mlp2_kernel, mode="ffn"),
        grid=(rows // tm, dff // tf),
        in_specs=[pl.BlockSpec((tm, dm), row_blk), pl.BlockSpec((tm, dm), row_blk),
                  pl.BlockSpec((dm, tf), lambda i, j: (0, j)), pl.BlockSpec((tf, dm), lambda i, j: (j, 0)), vec],
        out_specs=pl.BlockSpec((tm, dm), row_blk),
        out_shape=jax.ShapeDtypeStruct((rows, dm), F32),
        compiler_params=_params(("parallel", "arbitrary")),
        name="ffn",
    )(h, u, w1, w2, gpost)


def _rope_tables(pos):
    inv_freq = jnp.power(ROPE_THETA, -jnp.arange(0, ROPE_DIM, 2, dtype=F32) / ROPE_DIM)
    ang = pos[:, None] * inv_freq[None, :]
    cos, sin = jnp.cos(ang), jnp.sin(ang)
    n = pos.shape[0]
    pad = LANE - ROPE_DIM
    return (jnp.concatenate([cos, cos, jnp.ones((n, pad), F32)], axis=1),
            jnp.concatenate([-sin, sin, jnp.zeros((n, pad), F32)], axis=1))


def kernel(x_prompt, x_sample, cache_k, cache_v, state_gla, meta, norm_mix_pre, w_in, w_gla_a2, b_gla_a, gla_norm, diff_lq1, diff_lk1, diff_lq2, diff_lk2, diff_norm, w_br_gla, w_br_diff, w_o, norm_mix_post, norm_ffn_pre, w_ff1, w_ff2, norm_ffn_post):
    n_b, seq, dm = x_prompt.shape
    n_db, dec_seq, _ = x_sample.shape
    past = cache_k.shape[2]
    assert w_in.shape[0] == 1, "single-layer step only"
    assert dec_seq == CHUNK and seq % CHUNK == 0 and meta.shape[0] == N_META
    lam_init = 0.8 - 0.6 * math.exp(-0.3 * 0)

    sizes = (GLA_HEADS * GLA_DK, GLA_HEADS * GLA_DK, GLA_HEADS * GLA_DV, GATE_RANK, GLA_HEADS * GLA_DV,
             2 * DIFF_HEADS * DIFF_DQK, 2 * DIFF_HEADS * DIFF_DQK, DIFF_HEADS * DIFF_DV, dm, dm)
    offs = [0]
    for s in sizes:
        offs.append(offs[-1] + s)
    wi = w_in[0]
    col = lambda i: wi[:, offs[i]:offs[i + 1]]
    w_qk = jnp.concatenate([col(0), col(1)], axis=1).astype(BF16)
    w_vr = jnp.concatenate([col(2), col(4)], axis=1).astype(BF16)
    w_alow = jnp.pad(col(3), ((0, 0), (0, LANE - GATE_RANK))).astype(BF16)
    w_a2 = jnp.pad(w_gla_a2[0], ((0, LANE - GATE_RANK), (0, 0))).astype(BF16)
    w_q, w_k, w_v = col(5).astype(BF16), col(6).astype(BF16), col(7).astype(BF16)
    w_gate = jnp.concatenate([col(8), col(9)], axis=1).astype(BF16)
    b_a = b_gla_a
    g_pre = norm_mix_pre
    wa, wb, wo = w_br_gla[0].astype(BF16), w_br_diff[0].astype(BF16), w_o[0].astype(BF16)
    w1, w2 = w_ff1[0].astype(BF16), w_ff2[0].astype(BF16)
    lams = (diff_lq1, diff_lk1, diff_lq2, diff_lk2)

    xp = x_prompt.reshape(n_b * seq, dm)
    xs = x_sample.reshape(n_db * dec_seq, dm)
    xm = jnp.concatenate([jnp.zeros((META_LO, dm), x_prompt.dtype), meta.astype(x_prompt.dtype)], axis=0)
    cos_p, sin_p = _rope_tables(jnp.arange(seq, dtype=F32) + N_META)
    cos_s, sin_s = _rope_tables(jnp.tile(jnp.arange(dec_seq, dtype=F32) + past, n_db))
    cos_m, sin_m = _rope_tables(jnp.arange(META_ROWS, dtype=F32) - META_LO)

    def project(x, cos, sin, with_q, lead_rows=0, n_lead=1):
        u = _norm_call(x, g_pre)
        zqk, la = _proj_call("gla", u, w_qk, (w_alow, w_a2, b_a))
        zvr = _proj_call("cast", u, w_vr)
        kf, kb = _proj_call("k", u, w_k, (cos, sin), tm_pref=512, tn_pref=w_k.shape[1],
                            lead_rows=lead_rows, n_lead=n_lead)
        vf, vb = _proj_call("v", u, w_v, lead_rows=lead_rows, n_lead=n_lead)
        if not with_q:
            return zqk, zvr, la, kf, kb, vf, vb
        q = _proj_call("q", u, w_q, (cos, sin))
        gates = _proj_call("gate", u, w_gate)
        return zqk, zvr, la, kf, kb, vf, vb, q, gates

    zqk_m, zvr_m, la_m, kf_m, kb_m, vf_m, vb_m = project(xm, cos_m, sin_m, False)
    zero_state = jnp.zeros((1, GLA_HEADS, GLA_DK, GLA_DV), F32)
    _, s_meta = _gla_call(zqk_m, zvr_m, la_m, zero_state, gla_norm, 1)

    zqk, zvr, la, kf_p, kb_p, vf_p, vb_p, q_p, gates_p = project(xp, cos_p, sin_p, True, N_META, n_b)
    a_p, s_p = _gla_call(zqk, zvr, la, s_meta, gla_norm, n_b)
    key_pad = ((0, LANE - META_ROWS), (0, 0))
    d_p = _attn_prompt_call(q_p, kb_p, vb_p, jnp.pad(kb_m, key_pad), jnp.pad(vb_m, key_pad), lams, diff_norm,
                            n_b, lam_init)
    h_p, u_p = _merge_call(a_p, d_p, gates_p, wa, wb, wo, xp, norm_mix_post, norm_ffn_pre)
    y_p = _ffn_call(h_p, u_p, w1, w2, norm_ffn_post)

    zqk, zvr, la, kf_s, kb_s, vf_s, vb_s, q_s, gates_s = project(xs, cos_s, sin_s, True)
    a_s, s_s = _gla_call(zqk, zvr, la, state_gla[0].astype(F32), gla_norm, n_db)
    kc = cache_k[0].reshape(n_db, past * DIFF_HEADS * 2, DIFF_DQK)
    vc = cache_v[0].reshape(n_db, past * DIFF_HEADS, DIFF_DV)
    d_s = _attn_sample_call(q_s, kc, vc, kb_s, vb_s, lams, diff_norm, past, lam_init)
    h_s, u_s = _merge_call(a_s, d_s, gates_s, wa, wb, wo, xs, norm_mix_post, norm_ffn_pre)
    y_s = _ffn_call(h_s, u_s, w1, w2, norm_ffn_post)

    def with_meta(f_meta, f_real):
        per_frame = f_meta.shape[0] // META_ROWS
        m = f_meta[None, META_LO * per_frame:]
        return f_real.reshape(n_b, -1, f_real.shape[1]).at[:, :N_META * per_frame].set(
            jnp.broadcast_to(m, (n_b,) + m.shape[1:]))

    k_shape = (DIFF_HEADS, 2, DIFF_DQK)
    v_shape = (DIFF_HEADS, DIFF_DV)
    return (y_p.reshape(n_b, seq, dm),
            y_s.reshape(n_db, dec_seq, dm),
            with_meta(kf_m, kf_p).reshape((1, n_b, N_META + seq) + k_shape),
            with_meta(vf_m, vf_p).reshape((1, n_b, N_META + seq) + v_shape),
            s_p[None],
            kf_s.reshape((1, n_db, dec_seq) + k_shape),
            vf_s.reshape((1, n_db, dec_seq) + v_shape),
            s_s[None])
```

```python
import functools
import math

import jax
import jax.numpy as jnp
from jax import lax
from jax.experimental import pallas as pl
from jax.experimental.pallas import tpu as pltpu

F32 = jnp.float32
BF16 = jnp.bfloat16

EPS = 1e-6
CHUNK = 64
SUB = 16
N_SUB = CHUNK // SUB
N_META = 16
META_ROWS = CHUNK
META_LO = META_ROWS - N_META
GLA_HEADS = 4
GLA_DK = 256
GLA_DV = 512
GATE_RANK = 16
GLA_TAU = 16.0
DIFF_HEADS = 8
DIFF_DQK = 128
DIFF_DV = 256
ROPE_DIM = 32
ROPE_THETA = 500000.0
LANE = 128
QUERY_SCALE = DIFF_DQK ** -0.5 * math.log2(math.e)
EPILOGUE_ROWS = 128
MLP_ROWS = 256
MASKED = -1e30
VMEM_LIMIT = 56 * 1024 * 1024

_NT = (((1,), (1,)), ((), ()))
_TN = (((0,), (0,)), ((), ()))


def _tile(n, pref):
    return pref if n % pref == 0 else n


def _params(sem):
    return pltpu.CompilerParams(dimension_semantics=sem, vmem_limit_bytes=VMEM_LIMIT)


def _rms(x):
    return x * lax.rsqrt(jnp.mean(x * x, axis=-1, keepdims=True) + EPS)


def _rope(z, cos, sin):
    half = ROPE_DIM // 2
    lane = lax.broadcasted_iota(jnp.int32, (z.shape[0], LANE), 1)
    out = []
    for g in range(z.shape[1] // LANE):
        seg = z[:, g * LANE:(g + 1) * LANE]
        partner = jnp.where(lane < half, pltpu.roll(seg, LANE - half, 1), pltpu.roll(seg, half, 1))
        out.append(seg * cos + partner * sin)
    return jnp.concatenate(out, axis=1)


def _norm_kernel(x_ref, g_ref, u_ref):
    u_ref[...] = (_rms(x_ref[...]) * g_ref[...]).astype(BF16)


def _norm_call(x, g, tm_pref=512):
    rows, d = x.shape
    tm = _tile(rows, tm_pref)
    return pl.pallas_call(
        _norm_kernel,
        grid=(rows // tm,),
        in_specs=[pl.BlockSpec((tm, d), lambda i: (i, 0)), pl.BlockSpec((1, d), lambda i: (0, 0))],
        out_specs=pl.BlockSpec((tm, d), lambda i: (i, 0)),
        out_shape=jax.ShapeDtypeStruct((rows, d), BF16),
        compiler_params=_params(("parallel",)),
        name="norm",
    )(x, g)


def _proj_kernel(*refs, mode):
    if mode == "gla":
        u_ref, w_ref, wal_ref, wa2_ref, ba_ref, z_ref, la_ref = refs
    elif mode == "q":
        u_ref, w_ref, cos_ref, sin_ref, ob_ref = refs
    elif mode == "k":
        u_ref, w_ref, cos_ref, sin_ref, of_ref, ob_ref = refs
    elif mode == "v":
        u_ref, w_ref, of_ref, ob_ref = refs
    else:
        u_ref, w_ref, ob_ref = refs

    if mode == "gla":
        @pl.when(pl.program_id(1) == 0)
        def _():
            alow = jnp.dot(u_ref[...], wal_ref[...], preferred_element_type=F32)
            pre = jnp.dot(alow.astype(BF16), wa2_ref[...], preferred_element_type=F32) + ba_ref[...]
            la_ref[...] = (jnp.minimum(pre, 0.0) - jnp.log1p(jnp.exp(-jnp.abs(pre)))) * (1.0 / GLA_TAU)

    rr = min(EPILOGUE_ROWS, u_ref.shape[0])
    for r in range(u_ref.shape[0] // rr):
        rows = slice(r * rr, (r + 1) * rr)
        z = jnp.dot(u_ref[rows, :], w_ref[...], preferred_element_type=F32)
        if mode == "gla":
            z_ref[rows, :] = z
        elif mode == "q":
            ob_ref[rows, :] = (_rope(z, cos_ref[rows, :], sin_ref[rows, :]) * QUERY_SCALE).astype(BF16)
        elif mode == "k":
            zr = _rope(z, cos_ref[rows, :], sin_ref[rows, :])
            ob_ref[rows, :] = zr.astype(BF16)
            n_g = zr.shape[1] // LANE
            for g in range(n_g):
                of_ref[pl.ds(r * rr * n_g + g, rr, stride=n_g), :] = zr[:, g * LANE:(g + 1) * LANE]
        elif mode == "v":
            of_ref[rows, :] = z
            ob_ref[rows, :] = z.astype(BF16)
        elif mode == "gate":
            ob_ref[rows, :] = (1.0 / (1.0 + jnp.exp(-z))).astype(BF16)
        else:
            ob_ref[rows, :] = z.astype(BF16)


def _proj_call(mode, u, w, extra=(), tm_pref=2048, tn_pref=512, lead_rows=0, n_lead=1):
    rows, d = u.shape
    n = w.shape[1]
    tm, tn = _tile(rows // n_lead, tm_pref), _tile(n, tn_pref)
    tiles_g = rows // n_lead // tm
    row_blk = lambda i, j: (i, 0)
    fixed = lambda i, j: (0, 0)

    def lead_row(i, width):
        return pl.multiple_of(((i // tiles_g) * (tiles_g * tm + lead_rows) + lead_rows + (i % tiles_g) * tm) * width, 8)

    in_specs = [pl.BlockSpec((tm, d), row_blk), pl.BlockSpec((d, tn), lambda i, j: (0, j))]
    out_blk = pl.BlockSpec((tm, tn), lambda i, j: (i, j))
    if mode == "gla":
        wal, wa2, ba = extra
        in_specs += [pl.BlockSpec(wal.shape, fixed), pl.BlockSpec(wa2.shape, fixed), pl.BlockSpec(ba.shape, fixed)]
        out_shape = (jax.ShapeDtypeStruct((rows, n), F32), jax.ShapeDtypeStruct((rows, wa2.shape[1]), F32))
        out_specs = (out_blk, pl.BlockSpec((tm, wa2.shape[1]), row_blk))
    elif mode in ("q", "k"):
        table_blk = lambda i, j: (i % (extra[0].shape[0] // tm), 0)
        in_specs += [pl.BlockSpec((tm, LANE), table_blk), pl.BlockSpec((tm, LANE), table_blk)]
        if mode == "q":
            out_shape, out_specs = jax.ShapeDtypeStruct((rows, n), BF16), out_blk
        else:
            assert tn == n, "native-order key rows need every (head, map) group of a frame in one block"
            n_g = n // LANE
            out_shape = (jax.ShapeDtypeStruct(((rows + n_lead * lead_rows) * n_g, LANE), F32),
                         jax.ShapeDtypeStruct((rows, n), BF16))
            out_specs = (pl.BlockSpec((pl.Element(tm * n_g), pl.Element(LANE)), lambda i, j: (lead_row(i, n_g), 0)),
                         out_blk)
    elif mode == "v":
        out_shape = (jax.ShapeDtypeStruct((rows + n_lead * lead_rows, n), F32), jax.ShapeDtypeStruct((rows, n), BF16))
        out_specs = (pl.BlockSpec((pl.Element(tm), pl.Element(tn)),
                                  lambda i, j: (lead_row(i, 1), pl.multiple_of(j * tn, LANE))), out_blk)
    else:
        out_shape, out_specs = jax.ShapeDtypeStruct((rows, n), BF16), out_blk
    return pl.pallas_call(
        functools.partial(_proj_kernel, mode=mode),
        grid=(rows // tm, n // tn),
        in_specs=in_specs,
        out_specs=out_specs,
        out_shape=out_shape,
        compiler_params=_params(("parallel", "arbitrary")),
        name="proj_" + mode,
    )(u, w, *extra)


def _gla_kernel(zqk_ref, zvr_ref, la_ref, s0_ref, gn_ref, a_ref, sout_ref, st_ref, *, n_chunks):
    c = pl.program_id(1)

    @pl.when(c == 0)
    def _():
        for h in range(GLA_HEADS):
            st_ref[h] = s0_ref[h].T

    rsub = lax.broadcasted_iota(jnp.int32, (CHUNK, GLA_DK), 0) % SUB
    rr = lax.broadcasted_iota(jnp.int32, (CHUNK, N_SUB * CHUNK), 0)
    cc = lax.broadcasted_iota(jnp.int32, (CHUNK, N_SUB * CHUNK), 1)
    valid = jnp.where(cc // CHUNK == rr // SUB, cc % CHUNK, CHUNK) <= rr
    gn = gn_ref[...]
    zero = jnp.zeros((SUB, GLA_DK), F32)
    k_off = GLA_HEADS * GLA_DK
    r_off = GLA_HEADS * GLA_DV

    def prod(xs):
        out = xs[0]
        for x in xs[1:]:
            out = out * x
        return out

    def head_chunk(h, rows):
        q = zqk_ref[rows, h * GLA_DK:(h + 1) * GLA_DK] * (GLA_DK ** -0.5)
        k = zqk_ref[rows, k_off + h * GLA_DK:k_off + (h + 1) * GLA_DK]
        vb = zvr_ref[rows, h * GLA_DV:(h + 1) * GLA_DV]
        bl = la_ref[rows, h * GLA_DK:(h + 1) * GLA_DK]
        for sh in (1, 2, 4, 8):
            bl = bl + jnp.where(rsub >= sh, pltpu.roll(bl, sh, 0), 0.0)
        et = [jnp.exp(bl[SUB * j + SUB - 1:SUB * (j + 1), :]) for j in range(N_SUB)]
        qt = q * jnp.exp(bl)
        kd = k * jnp.exp(-bl)
        qs = [qt[SUB * j:SUB * (j + 1)] for j in range(N_SUB)]
        kds = [kd[SUB * j:SUB * (j + 1)] for j in range(N_SUB)]
        khat = [kds[j] * et[j] for j in range(N_SUB)]

        blocks = []
        for i in range(N_SUB):
            for j in range(N_SUB):
                if j < i:
                    blocks.append(khat[j] if j + 1 == i else khat[j] * prod(et[j + 1:i]))
                else:
                    blocks.append(kds[j] if j == i else zero)
        kstack = jnp.concatenate(blocks, axis=0).astype(BF16)
        sc = lax.dot_general(qt.astype(BF16), kstack, _NT, preferred_element_type=F32)
        p = jnp.where(valid, sc, 0.0).astype(BF16)
        intra = jnp.dot(p, jnp.concatenate([vb] * N_SUB, axis=0), preferred_element_type=F32)

        qd = jnp.concatenate([qs[i] if i == 0 else qs[i] * prod(et[:i]) for i in range(N_SUB)], axis=0)
        st = st_ref[h]
        inter = lax.dot_general(qd.astype(BF16), st.astype(BF16), _NT, preferred_element_type=F32)
        o = inter + intra

        kdec = jnp.concatenate(
            [khat[j] if j == N_SUB - 1 else khat[j] * prod(et[j + 1:]) for j in range(N_SUB)], axis=0)
        st_ref[h] = st * prod(et) + lax.dot_general(vb, kdec.astype(BF16), _TN, preferred_element_type=F32)

        r = zvr_ref[rows, r_off + h * GLA_DV:r_off + (h + 1) * GLA_DV].astype(F32)
        a_ref[rows, h * GLA_DV:(h + 1) * GLA_DV] = (_rms(o) * gn * (r / (1.0 + jnp.exp(-r)))).astype(BF16)

    def chunk(ci, carry):
        rows = pl.ds(pl.multiple_of(ci * CHUNK, CHUNK), CHUNK)
        for h in range(GLA_HEADS):
            head_chunk(h, rows)
        return carry

    lax.fori_loop(0, n_chunks, chunk, 0, unroll=True)

    @pl.when(c == pl.num_programs(1) - 1)
    def _():
        for h in range(GLA_HEADS):
            sout_ref[h] = st_ref[h].T


def _gla_call(zqk, zvr, la, s0, gn, n_batch, tc_pref=256):
    rows = zqk.shape[0]
    per_b = rows // n_batch
    tc = _tile(per_b, tc_pref)
    n_tc = per_b // tc
    s_stride = 0 if s0.shape[0] == 1 else 1
    row_blk = lambda b, c: (b * n_tc + c, 0)
    state_blk = (None, GLA_HEADS, GLA_DK, GLA_DV)
    return pl.pallas_call(
        functools.partial(_gla_kernel, n_chunks=tc // CHUNK),
        grid=(n_batch, n_tc),
        in_specs=[
            pl.BlockSpec((tc, zqk.shape[1]), row_blk),
            pl.BlockSpec((tc, zvr.shape[1]), row_blk),
            pl.BlockSpec((tc, la.shape[1]), row_blk),
            pl.BlockSpec(state_blk, lambda b, c: (b * s_stride, 0, 0, 0)),
            pl.BlockSpec((1, GLA_DV), lambda b, c: (0, 0)),
        ],
        out_specs=(
            pl.BlockSpec((tc, GLA_HEADS * GLA_DV), row_blk),
            pl.BlockSpec(state_blk, lambda b, c: (b, 0, 0, 0)),
        ),
        out_shape=(
            jax.ShapeDtypeStruct((rows, GLA_HEADS * GLA_DV), BF16),
            jax.ShapeDtypeStruct((n_batch, GLA_HEADS, GLA_DK, GLA_DV), F32),
        ),
        scratch_shapes=[pltpu.VMEM((GLA_HEADS, GLA_DV, GLA_DK), F32)],
        compiler_params=_params(("parallel", "arbitrary")),
        name="gla",
    )(zqk, zvr, la, s0, gn)


def _lam(lq1, lk1, lq2, lk2, lam_init):
    return (jnp.exp(jnp.sum(lq1[...] * lk1[...], axis=1, keepdims=True))
            - jnp.exp(jnp.sum(lq2[...] * lk2[...], axis=1, keepdims=True)) + lam_init)


def _scores(q, k):
    s0 = lax.dot_general(q[:, :DIFF_DQK], k[:, :DIFF_DQK], _NT, preferred_element_type=F32)
    s1 = lax.dot_general(q[:, DIFF_DQK:], k[:, DIFF_DQK:], _NT, preferred_element_type=F32)
    return jnp.concatenate([s0, s1], axis=0)


def _diff_out(acc0, acc1, l0, l1, lam, dn, lam_init):
    o = acc0 * (1.0 / l0) - lam * (acc1 * (1.0 / l1))
    return (_rms(o) * dn * (1.0 - lam_init)).astype(BF16)


def _attn_prompt_kernel(q_ref, k_ref, v_ref, kp_ref, vp_ref, lq1, lk1, lq2, lk2, dn_ref, o_ref,
                        m_ref, l_ref, acc_ref, *, tq, tk, rs, lam_init):
    qi = pl.program_id(2)

    def tile_update(mp, r, kblk, vblk, mask=None, first=False):
        rows = slice(r * rs, (r + 1) * rs)
        cols = slice(mp * DIFF_DQK, (mp + 1) * DIFF_DQK)
        s = lax.dot_general(q_ref[rows, cols], kblk[:, cols], _NT, preferred_element_type=F32)
        if mask is not None:
            s = jnp.where(mask, s, MASKED)
        parts = [s[:, c * LANE:(c + 1) * LANE] for c in range(s.shape[1] // LANE)]
        m_new = jnp.broadcast_to(jnp.max(functools.reduce(jnp.maximum, parts), axis=1, keepdims=True), (rs, LANE))
        if not first:
            m_prev = m_ref[mp, rows, :]
            m_new = jnp.maximum(m_prev, m_new)
            alpha = jnp.exp2(m_prev - m_new)
        ps = [jnp.exp2(x - m_new) for x in parts]
        l_new = functools.reduce(jnp.add, ps)
        pv = jnp.dot(jnp.concatenate(ps, axis=1).astype(BF16), vblk, preferred_element_type=F32)
        if not first:
            l_new = alpha * l_ref[mp, rows, :] + l_new
            pv = jnp.concatenate([alpha] * (DIFF_DV // LANE), axis=1) * acc_ref[mp, rows, :] + pv
        l_ref[mp, rows, :] = l_new
        acc_ref[mp, rows, :] = pv
        m_ref[mp, rows, :] = m_new

    n_r = tq // rs
    pcol = lax.broadcasted_iota(jnp.int32, (rs, kp_ref.shape[0]), 1)
    pmask = (pcol >= META_LO) & (pcol < META_ROWS)
    for mp in range(2):
        for r in range(n_r):
            tile_update(mp, r, kp_ref[...], vp_ref[...], pmask, first=True)

    def full_block(kb, carry):
        krows = pl.ds(pl.multiple_of(kb * tk, tk), tk)
        kblk, vblk = k_ref[krows, :], v_ref[krows, :]
        for mp in range(2):
            for r in range(n_r):
                tile_update(mp, r, kblk, vblk)
        return carry

    lax.fori_loop(0, qi * (tq // tk), full_block, 0)

    cmask = (lax.broadcasted_iota(jnp.int32, (rs, rs), 1) // CHUNK
             <= lax.broadcasted_iota(jnp.int32, (rs, rs), 0) // CHUNK)
    for r in range(n_r):
        for d in range(r + 1):
            krows = pl.ds(pl.multiple_of(qi * tq + d * rs, rs), rs)
            kblk, vblk = k_ref[krows, :], v_ref[krows, :]
            for mp in range(2):
                tile_update(mp, r, kblk, vblk, cmask if d == r else None)

    lam = _lam(lq1, lk1, lq2, lk2, lam_init)
    l = jnp.sum(l_ref[...], axis=2, keepdims=True)
    o_ref[...] = _diff_out(acc_ref[0], acc_ref[1], l[0], l[1], lam, dn_ref[...], lam_init)


def _attn_prompt_call(q, kb, vb, kmeta, vmeta, lams, dn, n_batch, lam_init, tq_pref=2048, tk_pref=512, rs_pref=256):
    rows, width = q.shape
    t = rows // n_batch
    tq = _tile(t, tq_pref)
    tk = _tile(tq, tk_pref)
    rs = _tile(tq, rs_pref)
    nq = t // tq
    hw = 2 * DIFF_DQK
    fixed = lambda b, h, i: (0, 0)
    kv_spec = pl.BlockSpec((t, hw), lambda b, h, i: (b, h))
    meta_spec = pl.BlockSpec((kmeta.shape[0], hw), lambda b, h, i: (0, h))
    vec = pl.BlockSpec((1, DIFF_DQK), fixed)
    return pl.pallas_call(
        functools.partial(_attn_prompt_kernel, tq=tq, tk=tk, rs=rs, lam_init=lam_init),
        grid=(n_batch, DIFF_HEADS, nq),
        in_specs=[pl.BlockSpec((tq, hw), lambda b, h, i: (b * nq + i, h)), kv_spec, kv_spec, meta_spec, meta_spec,
                  vec, vec, vec, vec, pl.BlockSpec((1, DIFF_DV), fixed)],
        out_specs=pl.BlockSpec((tq, DIFF_DV), lambda b, h, i: (b * nq + i, h)),
        out_shape=jax.ShapeDtypeStruct((rows, width), BF16),
        scratch_shapes=[pltpu.VMEM((2, tq, LANE), F32), pltpu.VMEM((2, tq, LANE), F32), pltpu.VMEM((2, tq, DIFF_DV), F32)],
        compiler_params=_params(("parallel", "parallel", "arbitrary")),
        name="attn_prompt",
    )(q, kb, vb, kmeta, vmeta, *lams, dn)


def _attn_sample_kernel(q_ref, kc_ref, vlo_ref, vhi_ref, kn_ref, vn_ref, lq1, lk1, lq2, lk2, dn_ref, o_ref, *,
                        past, lam_init):
    lam = _lam(lq1, lk1, lq2, lk2, lam_init)
    dn = dn_ref[...]
    hw = 2 * DIFF_DQK
    for h in range(DIFF_HEADS):
        cols = slice(h * hw, (h + 1) * hw)
        q = q_ref[:, cols]
        tq = q.shape[0]
        kc = jnp.concatenate(
            [kc_ref[pl.ds(2 * h + mp, past, stride=2 * DIFF_HEADS), :] for mp in range(2)], axis=1).astype(BF16)
        vc = jnp.concatenate(
            [ref[pl.ds(h, past, stride=DIFF_HEADS), :] for ref in (vlo_ref, vhi_ref)], axis=1).astype(BF16)
        sc = _scores(q, kc)
        sn = _scores(q, kn_ref[:, cols])
        m = jnp.maximum(jnp.max(sc, axis=1, keepdims=True), jnp.max(sn, axis=1, keepdims=True))
        pc = jnp.exp2(sc - m)
        pn = jnp.exp2(sn - m)
        l = jnp.sum(pc, axis=1, keepdims=True) + jnp.sum(pn, axis=1, keepdims=True)
        acc = (jnp.dot(pc.astype(BF16), vc, preferred_element_type=F32)
               + jnp.dot(pn.astype(BF16), vn_ref[:, cols], preferred_element_type=F32))
        o_ref[:, cols] = _diff_out(acc[:tq], acc[tq:], l[:tq], l[tq:], lam, dn, lam_init)


def _attn_sample_call(q, kcache, vcache, knew, vnew, lams, dn, past, lam_init):
    rows, width = q.shape
    n_batch = kcache.shape[0]
    t = rows // n_batch
    fixed = lambda b: (0, 0)
    new_spec = pl.BlockSpec((t, width), lambda b: (b, 0))
    vec = pl.BlockSpec((1, DIFF_DQK), fixed)
    return pl.pallas_call(
        functools.partial(_attn_sample_kernel, past=past, lam_init=lam_init),
        grid=(n_batch,),
        in_specs=[new_spec, pl.BlockSpec((None,) + kcache.shape[1:], lambda b: (b, 0, 0)),
                  pl.BlockSpec((None, vcache.shape[1], LANE), lambda b: (b, 0, 0)),
                  pl.BlockSpec((None, vcache.shape[1], LANE), lambda b: (b, 0, 1)), new_spec, new_spec,
                  vec, vec, vec, vec, pl.BlockSpec((1, DIFF_DV), fixed)],
        out_specs=new_spec,
        out_shape=jax.ShapeDtypeStruct((rows, width), BF16),
        compiler_params=_params(("parallel",)),
        name="attn_sample",
    )(q, kcache, vcache, vcache, knew, vnew, *lams, dn)


def _mlp2_kernel(*refs, mode):
    if mode == "merge":
        a_ref, d_ref, ga_ref, gb_ref, wa_ref, wb_ref, w2_ref, res_ref, gpost_ref, gnext_ref, o_ref, unext_ref = refs
    else:
        res_ref, u_ref, w1_ref, w2_ref, gpost_ref, o_ref = refs
    j = pl.program_id(1)

    @pl.when(j == 0)
    def _():
        o_ref[...] = jnp.zeros(o_ref.shape, F32)

    rr = min(MLP_ROWS, o_ref.shape[0])
    for r in range(o_ref.shape[0] // rr):
        rows = slice(r * rr, (r + 1) * rr)
        if mode == "merge":
            mid = (ga_ref[rows, :] * jnp.dot(a_ref[rows, :], wa_ref[...], preferred_element_type=F32)
                   + gb_ref[rows, :] * jnp.dot(d_ref[rows, :], wb_ref[...], preferred_element_type=F32))
        else:
            mid = jnp.square(jnp.maximum(jnp.dot(u_ref[rows, :], w1_ref[...], preferred_element_type=F32), 0.0))
        o_ref[rows, :] += jnp.dot(mid.astype(BF16), w2_ref[...], preferred_element_type=F32)

    @pl.when(j == pl.num_programs(1) - 1)
    def _():
        out = res_ref[...] + _rms(o_ref[...]) * gpost_ref[...]
        o_ref[...] = out
        if mode == "merge":
            unext_ref[...] = (_rms(out) * gnext_ref[...]).astype(BF16)


def _merge_call(a, d, gates, wa, wb, wo, res, gpost, gnext, tm_pref=512, tn_pref=512):
    rows, dm = res.shape
    k = a.shape[1]
    tm, tn = _tile(rows, tm_pref), _tile(dm, tn_pref)
    nj = dm // tn
    row_blk = lambda i, j: (i, 0)
    return pl.pallas_call(
        functools.partial(_mlp2_kernel, mode="merge"),
        grid=(rows // tm, nj),
        in_specs=[
            pl.BlockSpec((tm, k), row_blk), pl.BlockSpec((tm, k), row_blk),
            pl.BlockSpec((tm, tn), lambda i, j: (i, j)), pl.BlockSpec((tm, tn), lambda i, j: (i, nj + j)),
            pl.BlockSpec((k, tn), lambda i, j: (0, j)), pl.BlockSpec((k, tn), lambda i, j: (0, j)),
            pl.BlockSpec((tn, dm), lambda i, j: (j, 0)),
            pl.BlockSpec((tm, dm), row_blk), pl.BlockSpec((1, dm), lambda i, j: (0, 0)),
            pl.BlockSpec((1, dm), lambda i, j: (0, 0)),
        ],
        out_specs=(pl.BlockSpec((tm, dm), row_blk), pl.BlockSpec((tm, dm), row_blk)),
        out_shape=(jax.ShapeDtypeStruct((rows, dm), F32), jax.ShapeDtypeStruct((rows, dm), BF16)),
        compiler_params=_params(("parallel", "arbitrary")),
        name="merge",
    )(a, d, gates, gates, wa, wb, wo, res, gpost, gnext)


def _ffn_call(h, u, w1, w2, gpost, tm_pref=512, tf_pref=1024):
    rows, dm = h.shape
    dff = w1.shape[1]
    tm, tf = _tile(rows, tm_pref), _tile(dff, tf_pref)
    row_blk = lambda i, j: (i, 0)
    vec = pl.BlockSpec((1, dm), lambda i, j: (0, 0))
    return pl.pallas_call(
        functools.partial(_mlp2_kernel, mode="ffn"),
        grid=(rows // tm, dff // tf),
        in_specs=[pl.BlockSpec((tm, dm), row_blk), pl.BlockSpec((tm, dm), row_blk),
                  pl.BlockSpec((dm, tf), lambda i, j: (0, j)), pl.BlockSpec((tf, dm), lambda i, j: (j, 0)), vec],
        out_specs=pl.BlockSpec((tm, dm), row_blk),
        out_shape=jax.ShapeDtypeStruct((rows, dm), F32),
        compiler_params=_params(("parallel", "arbitrary")),
        name="ffn",
    )(h, u, w1, w2, gpost)


def _rope_tables(pos):
    inv_freq = jnp.power(ROPE_THETA, -jnp.arange(0, ROPE_DIM, 2, dtype=F32) / ROPE_DIM)
    ang = pos[:, None] * inv_freq[None, :]
    cos, sin = jnp.cos(ang), jnp.sin(ang)
    n = pos.shape[0]
    pad = LANE - ROPE_DIM
    return (jnp.concatenate([cos, cos, jnp.ones((n, pad), F32)], axis=1),
            jnp.concatenate([-sin, sin, jnp.zeros((n, pad), F32)], axis=1))


def kernel(x_prompt, x_sample, cache_k, cache_v, state_gla, meta, norm_mix_pre, w_in, w_gla_a2, b_gla_a, gla_norm, diff_lq1, diff_lk1, diff_lq2, diff_lk2, diff_norm, w_br_gla, w_br_diff, w_o, norm_mix_post, norm_ffn_pre, w_ff1, w_ff2, norm_ffn_post):
    n_b, seq, dm = x_prompt.shape
    n_db, dec_seq, _ = x_sample.shape
    past = cache_k.shape[2]
    assert w_in.shape[0] == 1, "single-layer step only"
    assert dec_seq == CHUNK and seq % CHUNK == 0 and meta.shape[0] == N_META
    lam_init = 0.8 - 0.6 * math.exp(-0.3 * 0)

    sizes = (GLA_HEADS * GLA_DK, GLA_HEADS * GLA_DK, GLA_HEADS * GLA_DV, GATE_RANK, GLA_HEADS * GLA_DV,
             2 * DIFF_HEADS * DIFF_DQK, 2 * DIFF_HEADS * DIFF_DQK, DIFF_HEADS * DIFF_DV, dm, dm)
    offs = [0]
    for s in sizes:
        offs.append(offs[-1] + s)
    wi = w_in[0]
    col = lambda i: wi[:, offs[i]:offs[i + 1]]
    w_qk = jnp.concatenate([col(0), col(1)], axis=1).astype(BF16)
    w_vr = jnp.concatenate([col(2), col(4)], axis=1).astype(BF16)
    w_alow = jnp.pad(col(3), ((0, 0), (0, LANE - GATE_RANK))).astype(BF16)
    w_a2 = jnp.pad(w_gla_a2[0], ((0, LANE - GATE_RANK), (0, 0))).astype(BF16)
    w_q, w_k, w_v = col(5).astype(BF16), col(6).astype(BF16), col(7).astype(BF16)
    w_gate = jnp.concatenate([col(8), col(9)], axis=1).astype(BF16)
    b_a = b_gla_a
    g_pre = norm_mix_pre
    wa, wb, wo = w_br_gla[0].astype(BF16), w_br_diff[0].astype(BF16), w_o[0].astype(BF16)
    w1, w2 = w_ff1[0].astype(BF16), w_ff2[0].astype(BF16)
    lams = (diff_lq1, diff_lk1, diff_lq2, diff_lk2)

    xp = x_prompt.reshape(n_b * seq, dm)
    xs = x_sample.reshape(n_db * dec_seq, dm)
    xm = jnp.concatenate([jnp.zeros((META_LO, dm), x_prompt.dtype), meta.astype(x_prompt.dtype)], axis=0)
    cos_p, sin_p = _rope_tables(jnp.arange(seq, dtype=F32) + N_META)
    cos_s, sin_s = _rope_tables(jnp.tile(jnp.arange(dec_seq, dtype=F32) + past, n_db))
    cos_m, sin_m = _rope_tables(jnp.arange(META_ROWS, dtype=F32) - META_LO)

    def project(x, cos, sin, with_q, lead_rows=0, n_lead=1):
        u = _norm_call(x, g_pre)
        zqk, la = _proj_call("gla", u, w_qk, (w_alow, w_a2, b_a))
        zvr = _proj_call("cast", u, w_vr)
        kf, kb = _proj_call("k", u, w_k, (cos, sin), tm_pref=512, tn_pref=w_k.shape[1],
                            lead_rows=lead_rows, n_lead=n_lead)
        vf, vb = _proj_call("v", u, w_v, lead_rows=lead_rows, n_lead=n_lead)
        if not with_q:
            return zqk, zvr, la, kf, kb, vf, vb
        q = _proj_call("q", u, w_q, (cos, sin))
        gates = _proj_call("gate", u, w_gate)
        return zqk, zvr, la, kf, kb, vf, vb, q, gates

    zqk_m, zvr_m, la_m, kf_m, kb_m, vf_m, vb_m = project(xm, cos_m, sin_m, False)
    zero_state = jnp.zeros((1, GLA_HEADS, GLA_DK, GLA_DV), F32)
    _, s_meta = _gla_call(zqk_m, zvr_m, la_m, zero_state, gla_norm, 1)

    zqk, zvr, la, kf_p, kb_p, vf_p, vb_p, q_p, gates_p = project(xp, cos_p, sin_p, True, N_META, n_b)
    a_p, s_p = _gla_call(zqk, zvr, la, s_meta, gla_norm, n_b)
    key_pad = ((0, LANE - META_ROWS), (0, 0))
    d_p = _attn_prompt_call(q_p, kb_p, vb_p, jnp.pad(kb_m, key_pad), jnp.pad(vb_m, key_pad), lams, diff_norm,
                            n_b, lam_init)
    h_p, u_p = _merge_call(a_p, d_p, gates_p, wa, wb, wo, xp, norm_mix_post, norm_ffn_pre)
    y_p = _ffn_call(h_p, u_p, w1, w2, norm_ffn_post)

    zqk, zvr, la, kf_s, kb_s, vf_s, vb_s, q_s, gates_s = project(xs, cos_s, sin_s, True)
    a_s, s_s = _gla_call(zqk, zvr, la, state_gla[0].astype(F32), gla_norm, n_db)
    kc = cache_k[0].reshape(n_db, past * DIFF_HEADS * 2, DIFF_DQK)
    vc = cache_v[0].reshape(n_db, past * DIFF_HEADS, DIFF_DV)
    d_s = _attn_sample_call(q_s, kc, vc, kb_s, vb_s, lams, diff_norm, past, lam_init)
    h_s, u_s = _merge_call(a_s, d_s, gates_s, wa, wb, wo, xs, norm_mix_post, norm_ffn_pre)
    y_s = _ffn_call(h_s, u_s, w1, w2, norm_ffn_post)

    def with_meta(f_meta, f_real):
        per_frame = f_meta.shape[0] // META_ROWS
        m = f_meta[None, META_LO * per_frame:]
        return f_real.reshape(n_b, -1, f_real.shape[1]).at[:, :N_META * per_frame].set(
            jnp.broadcast_to(m, (n_b,) + m.shape[1:]))

    k_shape = (DIFF_HEADS, 2, DIFF_DQK)
    v_shape = (DIFF_HEADS, DIFF_DV)
    return (y_p.reshape(n_b, seq, dm),
            y_s.reshape(n_db, dec_seq, dm),
            with_meta(kf_m, kf_p).reshape((1, n_b, N_META + seq) + k_shape),
            with_meta(vf_m, vf_p).reshape((1, n_b, N_META + seq) + v_shape),
            s_p[None],
            kf_s.reshape((1, n_db, dec_seq) + k_shape),
            vf_s.reshape((1, n_db, dec_seq) + v_shape),
            s_s[None])
```

```python
import functools
import math

import jax
import jax.numpy as jnp
from jax import lax
from jax.experimental import pallas as pl
from jax.experimental.pallas import tpu as pltpu

F32 = jnp.float32
BF16 = jnp.bfloat16

EPS = 1e-6
CHUNK = 64
SUB = 16
N_SUB = CHUNK // SUB
N_META = 16
META_ROWS = CHUNK
META_LO = META_ROWS - N_META
GLA_HEADS = 4
GLA_DK = 256
GLA_DV = 512
GATE_RANK = 16
GLA_TAU = 16.0
DIFF_HEADS = 8
DIFF_DQK = 128
DIFF_DV = 256
ROPE_DIM = 32
ROPE_THETA = 500000.0
LANE = 128
QUERY_SCALE = DIFF_DQK ** -0.5 * math.log2(math.e)
EPILOGUE_ROWS = 128
MLP_ROWS = 256
MASKED = -1e30
VMEM_LIMIT = 56 * 1024 * 1024

_NT = (((1,), (1,)), ((), ()))
_TN = (((0,), (0,)), ((), ()))


def _tile(n, pref):
    return pref if n % pref == 0 else n


def _params(sem):
    return pltpu.CompilerParams(dimension_semantics=sem, vmem_limit_bytes=VMEM_LIMIT)


def _rms(x):
    return x * lax.rsqrt(jnp.mean(x * x, axis=-1, keepdims=True) + EPS)


def _rope(z, cos, sin):
    half = ROPE_DIM // 2
    lane = lax.broadcasted_iota(jnp.int32, (z.shape[0], LANE), 1)
    out = []
    for g in range(z.shape[1] // LANE):
        seg = z[:, g * LANE:(g + 1) * LANE]
        partner = jnp.where(lane < half, pltpu.roll(seg, LANE - half, 1), pltpu.roll(seg, half, 1))
        out.append(seg * cos + partner * sin)
    return jnp.concatenate(out, axis=1)


def _norm_kernel(x_ref, g_ref, u_ref):
    u_ref[...] = (_rms(x_ref[...]) * g_ref[...]).astype(BF16)


def _norm_call(x, g, tm_pref=512):
    rows, d = x.shape
    tm = _tile(rows, tm_pref)
    return pl.pallas_call(
        _norm_kernel,
        grid=(rows // tm,),
        in_specs=[pl.BlockSpec((tm, d), lambda i: (i, 0)), pl.BlockSpec((1, d), lambda i: (0, 0))],
        out_specs=pl.BlockSpec((tm, d), lambda i: (i, 0)),
        out_shape=jax.ShapeDtypeStruct((rows, d), BF16),
        compiler_params=_params(("parallel",)),
        name="norm",
    )(x, g)


def _proj_kernel(*refs, mode):
    if mode == "gla":
        u_ref, w_ref, wal_ref, wa2_ref, ba_ref, z_ref, la_ref = refs
    elif mode == "q":
        u_ref, w_ref, cos_ref, sin_ref, ob_ref = refs
    elif mode == "k":
        u_ref, w_ref, cos_ref, sin_ref, of_ref, ob_ref = refs
    elif mode == "v":
        u_ref, w_ref, of_ref, ob_ref = refs
    else:
        u_ref, w_ref, ob_ref = refs

    if mode == "gla":
        @pl.when(pl.program_id(1) == 0)
        def _():
            alow = jnp.dot(u_ref[...], wal_ref[...], preferred_element_type=F32)
            pre = jnp.dot(alow.astype(BF16), wa2_ref[...], preferred_element_type=F32) + ba_ref[...]
            la_ref[...] = (jnp.minimum(pre, 0.0) - jnp.log1p(jnp.exp(-jnp.abs(pre)))) * (1.0 / GLA_TAU)

    rr = min(EPILOGUE_ROWS, u_ref.shape[0])
    for r in range(u_ref.shape[0] // rr):
        rows = slice(r * rr, (r + 1) * rr)
        z = jnp.dot(u_ref[rows, :], w_ref[...], preferred_element_type=F32)
        if mode == "gla":
            z_ref[rows, :] = z
        elif mode == "q":
            ob_ref[rows, :] = (_rope(z, cos_ref[rows, :], sin_ref[rows, :]) * QUERY_SCALE).astype(BF16)
        elif mode == "k":
            zr = _rope(z, cos_ref[rows, :], sin_ref[rows, :])
            ob_ref[rows, :] = zr.astype(BF16)
            n_g = zr.shape[1] // LANE
            for g in range(n_g):
                of_ref[pl.ds(r * rr * n_g + g, rr, stride=n_g), :] = zr[:, g * LANE:(g + 1) * LANE]
        elif mode == "v":
            of_ref[rows, :] = z
            ob_ref[rows, :] = z.astype(BF16)
        elif mode == "gate":
            ob_ref[rows, :] = (1.0 / (1.0 + jnp.exp(-z))).astype(BF16)
        else:
            ob_ref[rows, :] = z.astype(BF16)


def _proj_call(mode, u, w, extra=(), tm_pref=2048, tn_pref=512, lead_rows=0, n_lead=1):
    rows, d = u.shape
    n = w.shape[1]
    tm, tn = _tile(rows // n_lead, tm_pref), _tile(n, tn_pref)
    tiles_g = rows // n_lead // tm
    row_blk = lambda i, j: (i, 0)
    fixed = lambda i, j: (0, 0)

    def lead_row(i, width):
        return pl.multiple_of(((i // tiles_g) * (tiles_g * tm + lead_rows) + lead_rows + (i % tiles_g) * tm) * width, 8)

    in_specs = [pl.BlockSpec((tm, d), row_blk), pl.BlockSpec((d, tn), lambda i, j: (0, j))]
    out_blk = pl.BlockSpec((tm, tn), lambda i, j: (i, j))
    if mode == "gla":
        wal, wa2, ba = extra
        in_specs += [pl.BlockSpec(wal.shape, fixed), pl.BlockSpec(wa2.shape, fixed), pl.BlockSpec(ba.shape, fixed)]
        out_shape = (jax.ShapeDtypeStruct((rows, n), F32), jax.ShapeDtypeStruct((rows, wa2.shape[1]), F32))
        out_specs = (out_blk, pl.BlockSpec((tm, wa2.shape[1]), row_blk))
    elif mode in ("q", "k"):
        table_blk = lambda i, j: (i % (extra[0].shape[0] // tm), 0)
        in_specs += [pl.BlockSpec((tm, LANE), table_blk), pl.BlockSpec((tm, LANE), table_blk)]
        if mode == "q":
            out_shape, out_specs = jax.ShapeDtypeStruct((rows, n), BF16), out_blk
        else:
            assert tn == n, "native-order key rows need every (head, map) group of a frame in one block"
            n_g = n // LANE
            out_shape = (jax.ShapeDtypeStruct(((rows + n_lead * lead_rows) * n_g, LANE), F32),
                         jax.ShapeDtypeStruct((rows, n), BF16))
            out_specs = (pl.BlockSpec((pl.Element(tm * n_g), pl.Element(LANE)), lambda i, j: (lead_row(i, n_g), 0)),
                         out_blk)
    elif mode == "v":
        out_shape = (jax.ShapeDtypeStruct((rows + n_lead * lead_rows, n), F32), jax.ShapeDtypeStruct((rows, n), BF16))
        out_specs = (pl.BlockSpec((pl.Element(tm), pl.Element(tn)),
                                  lambda i, j: (lead_row(i, 1), pl.multiple_of(j * tn, LANE))), out_blk)
    else:
        out_shape, out_specs = jax.ShapeDtypeStruct((rows, n), BF16), out_blk
    return pl.pallas_call(
        functools.partial(_proj_kernel, mode=mode),
        grid=(rows // tm, n // tn),
        in_specs=in_specs,
        out_specs=out_specs,
        out_shape=out_shape,
        compiler_params=_params(("parallel", "arbitrary")),
        name="proj_" + mode,
    )(u, w, *extra)


def _gla_kernel(zqk_ref, zvr_ref, la_ref, s0_ref, gn_ref, a_ref, sout_ref, st_ref, *, n_chunks):
    c = pl.program_id(1)

    @pl.when(c == 0)
    def _():
        for h in range(GLA_HEADS):
            st_ref[h] = s0_ref[h].T

    rsub = lax.broadcasted_iota(jnp.int32, (CHUNK, GLA_DK), 0) % SUB
    rr = lax.broadcasted_iota(jnp.int32, (CHUNK, N_SUB * CHUNK), 0)
    cc = lax.broadcasted_iota(jnp.int32, (CHUNK, N_SUB * CHUNK), 1)
    valid = jnp.where(cc // CHUNK == rr // SUB, cc % CHUNK, CHUNK) <= rr
    gn = gn_ref[...]
    zero = jnp.zeros((SUB, GLA_DK), F32)
    k_off = GLA_HEADS * GLA_DK
    r_off = GLA_HEADS * GLA_DV

    def prod(xs):
        out = xs[0]
        for x in xs[1:]:
            out = out * x
        return out

    def head_chunk(h, rows):
        q = zqk_ref[rows, h * GLA_DK:(h + 1) * GLA_DK] * (GLA_DK ** -0.5)
        k = zqk_ref[rows, k_off + h * GLA_DK:k_off + (h + 1) * GLA_DK]
        vb = zvr_ref[rows, h * GLA_DV:(h + 1) * GLA_DV]
        bl = la_ref[rows, h * GLA_DK:(h + 1) * GLA_DK]
        for sh in (1, 2, 4, 8):
            bl = bl + jnp.where(rsub >= sh, pltpu.roll(bl, sh, 0), 0.0)
        et = [jnp.exp(bl[SUB * j + SUB - 1:SUB * (j + 1), :]) for j in range(N_SUB)]
        qt = q * jnp.exp(bl)
        kd = k * jnp.exp(-bl)
        qs = [qt[SUB * j:SUB * (j + 1)] for j in range(N_SUB)]
        kds = [kd[SUB * j:SUB * (j + 1)] for j in range(N_SUB)]
        khat = [kds[j] * et[j] for j in range(N_SUB)]

        blocks = []
        for i in range(N_SUB):
            for j in range(N_SUB):
                if j < i:
                    blocks.append(khat[j] if j + 1 == i else khat[j] * prod(et[j + 1:i]))
                else:
                    blocks.append(kds[j] if j == i else zero)
        kstack = jnp.concatenate(blocks, axis=0).astype(BF16)
        sc = lax.dot_general(qt.astype(BF16), kstack, _NT, preferred_element_type=F32)
        p = jnp.where(valid, sc, 0.0).astype(BF16)
        intra = jnp.dot(p, jnp.concatenate([vb] * N_SUB, axis=0), preferred_element_type=F32)

        qd = jnp.concatenate([qs[i] if i == 0 else qs[i] * prod(et[:i]) for i in range(N_SUB)], axis=0)
        st = st_ref[h]
        inter = lax.dot_general(qd.astype(BF16), st.astype(BF16), _NT, preferred_element_type=F32)
        o = inter + intra

        kdec = jnp.concatenate(
            [khat[j] if j == N_SUB - 1 else khat[j] * prod(et[j + 1:]) for j in range(N_SUB)], axis=0)
        st_ref[h] = st * prod(et) + lax.dot_general(vb, kdec.astype(BF16), _TN, preferred_element_type=F32)

        r = zvr_ref[rows, r_off + h * GLA_DV:r_off + (h + 1) * GLA_DV].astype(F32)
        a_ref[rows, h * GLA_DV:(h + 1) * GLA_DV] = (_rms(o) * gn * (r / (1.0 + jnp.exp(-r)))).astype(BF16)

    def chunk(ci, carry):
        rows = pl.ds(pl.multiple_of(ci * CHUNK, CHUNK), CHUNK)
        for h in range(GLA_HEADS):
            head_chunk(h, rows)
        return carry

    lax.fori_loop(0, n_chunks, chunk, 0, unroll=True)

    @pl.when(c == pl.num_programs(1) - 1)
    def _():
        for h in range(GLA_HEADS):
            sout_ref[h] = st_ref[h].T


def _gla_call(zqk, zvr, la, s0, gn, n_batch, tc_pref=256):
    rows = zqk.shape[0]
    per_b = rows // n_batch
    tc = _tile(per_b, tc_pref)
    n_tc = per_b // tc
    s_stride = 0 if s0.shape[0] == 1 else 1
    row_blk = lambda b, c: (b * n_tc + c, 0)
    state_blk = (None, GLA_HEADS, GLA_DK, GLA_DV)
    return pl.pallas_call(
        functools.partial(_gla_kernel, n_chunks=tc // CHUNK),
        grid=(n_batch, n_tc),
        in_specs=[
            pl.BlockSpec((tc, zqk.shape[1]), row_blk),
            pl.BlockSpec((tc, zvr.shape[1]), row_blk),
            pl.BlockSpec((tc, la.shape[1]), row_blk),
            pl.BlockSpec(state_blk, lambda b, c: (b * s_stride, 0, 0, 0)),
            pl.BlockSpec((1, GLA_DV), lambda b, c: (0, 0)),
        ],
        out_specs=(
            pl.BlockSpec((tc, GLA_HEADS * GLA_DV), row_blk),
            pl.BlockSpec(state_blk, lambda b, c: (b, 0, 0, 0)),
        ),
        out_shape=(
            jax.ShapeDtypeStruct((rows, GLA_HEADS * GLA_DV), BF16),
            jax.ShapeDtypeStruct((n_batch, GLA_HEADS, GLA_DK, GLA_DV), F32),
        ),
        scratch_shapes=[pltpu.VMEM((GLA_HEADS, GLA_DV, GLA_DK), F32)],
        compiler_params=_params(("parallel", "arbitrary")),
        name="gla",
    )(zqk, zvr, la, s0, gn)


def _lam(lq1, lk1, lq2, lk2, lam_init):
    return (jnp.exp(jnp.sum(lq1[...] * lk1[...], axis=1, keepdims=True))
            - jnp.exp(jnp.sum(lq2[...] * lk2[...], axis=1, keepdims=True)) + lam_init)


def _scores(q, k):
    s0 = lax.dot_general(q[:, :DIFF_DQK], k[:, :DIFF_DQK], _NT, preferred_element_type=F32)
    s1 = lax.dot_general(q[:, DIFF_DQK:], k[:, DIFF_DQK:], _NT, preferred_element_type=F32)
    return jnp.concatenate([s0, s1], axis=0)


def _diff_out(acc0, acc1, l0, l1, lam, dn, lam_init):
    o = acc0 * (1.0 / l0) - lam * (acc1 * (1.0 / l1))
    return (_rms(o) * dn * (1.0 - lam_init)).astype(BF16)


def _attn_prompt_kernel(q_ref, k_ref, v_ref, kp_ref, vp_ref, lq1, lk1, lq2, lk2, dn_ref, o_ref,
                        m_ref, l_ref, acc_ref, *, tq, tk, rs, lam_init):
    qi = pl.program_id(2)

    def tile_update(mp, r, kblk, vblk, mask=None, first=False):
        rows = slice(r * rs, (r + 1) * rs)
        cols = slice(mp * DIFF_DQK, (mp + 1) * DIFF_DQK)
        s = lax.dot_general(q_ref[rows, cols], kblk[:, cols], _NT, preferred_element_type=F32)
        if mask is not None:
            s = jnp.where(mask, s, MASKED)
        parts = [s[:, c * LANE:(c + 1) * LANE] for c in range(s.shape[1] // LANE)]
        m_new = jnp.broadcast_to(jnp.max(functools.reduce(jnp.maximum, parts), axis=1, keepdims=True), (rs, LANE))
        if not first:
            m_prev = m_ref[mp, rows, :]
            m_new = jnp.maximum(m_prev, m_new)
            alpha = jnp.exp2(m_prev - m_new)
        ps = [jnp.exp2(x - m_new) for x in parts]
        l_new = functools.reduce(jnp.add, ps)
        pv = jnp.dot(jnp.concatenate(ps, axis=1).astype(BF16), vblk, preferred_element_type=F32)
        if not first:
            l_new = alpha * l_ref[mp, rows, :] + l_new
            pv = jnp.concatenate([alpha] * (DIFF_DV // LANE), axis=1) * acc_ref[mp, rows, :] + pv
        l_ref[mp, rows, :] = l_new
        acc_ref[mp, rows, :] = pv
        m_ref[mp, rows, :] = m_new

    n_r = tq // rs
    pcol = lax.broadcasted_iota(jnp.int32, (rs, kp_ref.shape[0]), 1)
    pmask = (pcol >= META_LO) & (pcol < META_ROWS)
    for mp in range(2):
        for r in range(n_r):
            tile_update(mp, r, kp_ref[...], vp_ref[...], pmask, first=True)

    def full_block(kb, carry):
        krows = pl.ds(pl.multiple_of(kb * tk, tk), tk)
        kblk, vblk = k_ref[krows, :], v_ref[krows, :]
        for mp in range(2):
            for r in range(n_r):
                tile_update(mp, r, kblk, vblk)
        return carry

    lax.fori_loop(0, qi * (tq // tk), full_block, 0)

    cmask = (lax.broadcasted_iota(jnp.int32, (rs, rs), 1) // CHUNK
             <= lax.broadcasted_iota(jnp.int32, (rs, rs), 0) // CHUNK)
    for r in range(n_r):
        for d in range(r + 1):
            krows = pl.ds(pl.multiple_of(qi * tq + d * rs, rs), rs)
            kblk, vblk = k_ref[krows, :], v_ref[krows, :]
            for mp in range(2):
                tile_update(mp, r, kblk, vblk, cmask if d == r else None)

    lam = _lam(lq1, lk1, lq2, lk2, lam_init)
    l = jnp.sum(l_ref[...], axis=2, keepdims=True)
    o_ref[...] = _diff_out(acc_ref[0], acc_ref[1], l[0], l[1], lam, dn_ref[...], lam_init)


def _attn_prompt_call(q, kb, vb, kmeta, vmeta, lams, dn, n_batch, lam_init, tq_pref=2048, tk_pref=512, rs_pref=256):
    rows, width = q.shape
    t = rows // n_batch
    tq = _tile(t, tq_pref)
    tk = _tile(tq, tk_pref)
    rs = _tile(tq, rs_pref)
    nq = t // tq
    hw = 2 * DIFF_DQK
    fixed = lambda b, h, i: (0, 0)
    kv_spec = pl.BlockSpec((t, hw), lambda b, h, i: (b, h))
    meta_spec = pl.BlockSpec((kmeta.shape[0], hw), lambda b, h, i: (0, h))
    vec = pl.BlockSpec((1, DIFF_DQK), fixed)
    return pl.pallas_call(
        functools.partial(_attn_prompt_kernel, tq=tq, tk=tk, rs=rs, lam_init=lam_init),
        grid=(n_batch, DIFF_HEADS, nq),
        in_specs=[pl.BlockSpec((tq, hw), lambda b, h, i: (b * nq + i, h)), kv_spec, kv_spec, meta_spec, meta_spec,
                  vec, vec, vec, vec, pl.BlockSpec((1, DIFF_DV), fixed)],
        out_specs=pl.BlockSpec((tq, DIFF_DV), lambda b, h, i: (b * nq + i, h)),
        out_shape=jax.ShapeDtypeStruct((rows, width), BF16),
        scratch_shapes=[pltpu.VMEM((2, tq, LANE), F32), pltpu.VMEM((2, tq, LANE), F32), pltpu.VMEM((2, tq, DIFF_DV), F32)],
        compiler_params=_params(("parallel", "parallel", "arbitrary")),
        name="attn_prompt",
    )(q, kb, vb, kmeta, vmeta, *lams, dn)


def _attn_sample_kernel(q_ref, kc_ref, vlo_ref, vhi_ref, kn_ref, vn_ref, lq1, lk1, lq2, lk2, dn_ref, o_ref, *,
                        past, lam_init):
    lam = _lam(lq1, lk1, lq2, lk2, lam_init)
    dn = dn_ref[...]
    hw = 2 * DIFF_DQK
    for h in range(DIFF_HEADS):
        cols = slice(h * hw, (h + 1) * hw)
        q = q_ref[:, cols]
        tq = q.shape[0]
        kc = jnp.concatenate(
            [kc_ref[pl.ds(2 * h + mp, past, stride=2 * DIFF_HEADS), :] for mp in range(2)], axis=1).astype(BF16)
        vc = jnp.concatenate(
            [ref[pl.ds(h, past, stride=DIFF_HEADS), :] for ref in (vlo_ref, vhi_ref)], axis=1).astype(BF16)
        sc = _scores(q, kc)
        sn = _scores(q, kn_ref[:, cols])
        m = jnp.maximum(jnp.max(sc, axis=1, keepdims=True), jnp.max(sn, axis=1, keepdims=True))
        pc = jnp.exp2(sc - m)
        pn = jnp.exp2(sn - m)
        l = jnp.sum(pc, axis=1, keepdims=True) + jnp.sum(pn, axis=1, keepdims=True)
        acc = (jnp.dot(pc.astype(BF16), vc, preferred_element_type=F32)
               + jnp.dot(pn.astype(BF16), vn_ref[:, cols], preferred_element_type=F32))
        o_ref[:, cols] = _diff_out(acc[:tq], acc[tq:], l[:tq], l[tq:], lam, dn, lam_init)


def _attn_sample_call(q, kcache, vcache, knew, vnew, lams, dn, past, lam_init):
    rows, width = q.shape
    n_batch = kcache.shape[0]
    t = rows // n_batch
    fixed = lambda b: (0, 0)
    new_spec = pl.BlockSpec((t, width), lambda b: (b, 0))
    vec = pl.BlockSpec((1, DIFF_DQK), fixed)
    return pl.pallas_call(
        functools.partial(_attn_sample_kernel, past=past, lam_init=lam_init),
        grid=(n_batch,),
        in_specs=[new_spec, pl.BlockSpec((None,) + kcache.shape[1:], lambda b: (b, 0, 0)),
                  pl.BlockSpec((None, vcache.shape[1], LANE), lambda b: (b, 0, 0)),
                  pl.BlockSpec((None, vcache.shape[1], LANE), lambda b: (b, 0, 1)), new_spec, new_spec,
                  vec, vec, vec, vec, pl.BlockSpec((1, DIFF_DV), fixed)],
        out_specs=new_spec,
        out_shape=jax.ShapeDtypeStruct((rows, width), BF16),
        compiler_params=_params(("parallel",)),
        name="attn_sample",
    )(q, kcache, vcache, vcache, knew, vnew, *lams, dn)


def _mix_kernel(a_ref, d_ref, ga_ref, gb_ref, wa_ref, wb_ref, o_ref):
    rr = min(MLP_ROWS, o_ref.shape[0])
    for r in range(o_ref.shape[0] // rr):
        rows = slice(r * rr, (r + 1) * rr)
        mix = (ga_ref[rows, :] * jnp.dot(a_ref[rows, :], wa_ref[...], preferred_element_type=F32)
               + gb_ref[rows, :] * jnp.dot(d_ref[rows, :], wb_ref[...], preferred_element_type=F32))
        o_ref[rows, :] = mix.astype(BF16)


def _mix_call(a, d, gates, wa, wb, tm_pref=1024, tn_pref=512):
    rows, k = a.shape
    n = wa.shape[1]
    tm, tn = _tile(rows, tm_pref), _tile(n, tn_pref)
    nj = n // tn
    row_blk = lambda i, j: (i, 0)
    return pl.pallas_call(
        _mix_kernel,
        grid=(rows // tm, nj),
        in_specs=[pl.BlockSpec((tm, k), row_blk), pl.BlockSpec((tm, k), row_blk),
                  pl.BlockSpec((tm, tn), lambda i, j: (i, j)), pl.BlockSpec((tm, tn), lambda i, j: (i, nj + j)),
                  pl.BlockSpec((k, tn), lambda i, j: (0, j)), pl.BlockSpec((k, tn), lambda i, j: (0, j))],
        out_specs=pl.BlockSpec((tm, tn), lambda i, j: (i, j)),
        out_shape=jax.ShapeDtypeStruct((rows, n), BF16),
        compiler_params=_params(("parallel", "arbitrary")),
        name="mix",
    )(a, d, gates, gates, wa, wb)


def _mixout_kernel(mix_ref, wo_ref, res_ref, gpost_ref, gnext_ref, h_ref, u_ref):
    rr = min(EPILOGUE_ROWS, h_ref.shape[0])
    for r in range(h_ref.shape[0] // rr):
        rows = slice(r * rr, (r + 1) * rr)
        t = jnp.dot(mix_ref[rows, :], wo_ref[...], preferred_element_type=F32)
        h = res_ref[rows, :] + _rms(t) * gpost_ref[...]
        h_ref[rows, :] = h
        u_ref[rows, :] = (_rms(h) * gnext_ref[...]).astype(BF16)


def _mixout_call(mix, wo, res, gpost, gnext, tm_pref=512):
    rows, dm = res.shape
    tm = _tile(rows, tm_pref)
    row_blk = lambda i: (i, 0)
    vec = pl.BlockSpec((1, dm), lambda i: (0, 0))
    return pl.pallas_call(
        _mixout_kernel,
        grid=(rows // tm,),
        in_specs=[pl.BlockSpec((tm, mix.shape[1]), row_blk), pl.BlockSpec(wo.shape, lambda i: (0, 0)),
                  pl.BlockSpec((tm, dm), row_blk), vec, vec],
        out_specs=(pl.BlockSpec((tm, dm), row_blk), pl.BlockSpec((tm, dm), row_blk)),
        out_shape=(jax.ShapeDtypeStruct((rows, dm), F32), jax.ShapeDtypeStruct((rows, dm), BF16)),
        compiler_params=_params(("parallel",)),
        name="mixout",
    )(mix, wo, res, gpost, gnext)


def _ffn_kernel(res_ref, u_ref, w1_ref, w2_ref, gpost_ref, o_ref):
    j = pl.program_id(1)

    @pl.when(j == 0)
    def _():
        o_ref[...] = jnp.zeros(o_ref.shape, F32)

    rr = min(MLP_ROWS, o_ref.shape[0])
    for r in range(o_ref.shape[0] // rr):
        rows = slice(r * rr, (r + 1) * rr)
        mid = jnp.square(jnp.maximum(jnp.dot(u_ref[rows, :], w1_ref[...], preferred_element_type=F32), 0.0))
        o_ref[rows, :] += jnp.dot(mid.astype(BF16), w2_ref[...], preferred_element_type=F32)

    @pl.when(j == pl.num_programs(1) - 1)
    def _():
        o_ref[...] = res_ref[...] + _rms(o_ref[...]) * gpost_ref[...]


def _ffn_call(h, u, w1, w2, gpost, tm_pref=512, tf_pref=1024):
    rows, dm = h.shape
    dff = w1.shape[1]
    tm, tf = _tile(rows, tm_pref), _tile(dff, tf_pref)
    row_blk = lambda i, j: (i, 0)
    vec = pl.BlockSpec((1, dm), lambda i, j: (0, 0))
    return pl.pallas_call(
        _ffn_kernel,
        grid=(rows // tm, dff // tf),
        in_specs=[pl.BlockSpec((tm, dm), row_blk), pl.BlockSpec((tm, dm), row_blk),
                  pl.BlockSpec((dm, tf), lambda i, j: (0, j)), pl.BlockSpec((tf, dm), lambda i, j: (j, 0)), vec],
        out_specs=pl.BlockSpec((tm, dm), row_blk),
        out_shape=jax.ShapeDtypeStruct((rows, dm), F32),
        compiler_params=_params(("parallel", "arbitrary")),
        name="ffn",
    )(h, u, w1, w2, gpost)


def _rope_tables(pos):
    inv_freq = jnp.power(ROPE_THETA, -jnp.arange(0, ROPE_DIM, 2, dtype=F32) / ROPE_DIM)
    ang = pos[:, None] * inv_freq[None, :]
    cos, sin = jnp.cos(ang), jnp.sin(ang)
    n = pos.shape[0]
    pad = LANE - ROPE_DIM
    return (jnp.concatenate([cos, cos, jnp.ones((n, pad), F32)], axis=1),
            jnp.concatenate([-sin, sin, jnp.zeros((n, pad), F32)], axis=1))


def kernel(x_prompt, x_sample, cache_k, cache_v, state_gla, meta, norm_mix_pre, w_in, w_gla_a2, b_gla_a, gla_norm, diff_lq1, diff_lk1, diff_lq2, diff_lk2, diff_norm, w_br_gla, w_br_diff, w_o, norm_mix_post, norm_ffn_pre, w_ff1, w_ff2, norm_ffn_post):
    n_b, seq, dm = x_prompt.shape
    n_db, dec_seq, _ = x_sample.shape
    past = cache_k.shape[2]
    assert w_in.shape[0] == 1, "single-layer step only"
    assert dec_seq == CHUNK and seq % CHUNK == 0 and meta.shape[0] == N_META
    lam_init = 0.8 - 0.6 * math.exp(-0.3 * 0)

    sizes = (GLA_HEADS * GLA_DK, GLA_HEADS * GLA_DK, GLA_HEADS * GLA_DV, GATE_RANK, GLA_HEADS * GLA_DV,
             2 * DIFF_HEADS * DIFF_DQK, 2 * DIFF_HEADS * DIFF_DQK, DIFF_HEADS * DIFF_DV, dm, dm)
    offs = [0]
    for s in sizes:
        offs.append(offs[-1] + s)
    wi = w_in[0]
    col = lambda i: wi[:, offs[i]:offs[i + 1]]
    w_qk = jnp.concatenate([col(0), col(1)], axis=1).astype(BF16)
    w_vr = jnp.concatenate([col(2), col(4)], axis=1).astype(BF16)
    w_alow = jnp.pad(col(3), ((0, 0), (0, LANE - GATE_RANK))).astype(BF16)
    w_a2 = jnp.pad(w_gla_a2[0], ((0, LANE - GATE_RANK), (0, 0))).astype(BF16)
    w_q, w_k, w_v = col(5).astype(BF16), col(6).astype(BF16), col(7).astype(BF16)
    w_gate = jnp.concatenate([col(8), col(9)], axis=1).astype(BF16)
    b_a = b_gla_a
    g_pre = norm_mix_pre
    wa, wb, wo = w_br_gla[0].astype(BF16), w_br_diff[0].astype(BF16), w_o[0].astype(BF16)
    w1, w2 = w_ff1[0].astype(BF16), w_ff2[0].astype(BF16)
    lams = (diff_lq1, diff_lk1, diff_lq2, diff_lk2)

    xp = x_prompt.reshape(n_b * seq, dm)
    xs = x_sample.reshape(n_db * dec_seq, dm)
    xm = jnp.concatenate([jnp.zeros((META_LO, dm), x_prompt.dtype), meta.astype(x_prompt.dtype)], axis=0)
    cos_p, sin_p = _rope_tables(jnp.arange(seq, dtype=F32) + N_META)
    cos_s, sin_s = _rope_tables(jnp.tile(jnp.arange(dec_seq, dtype=F32) + past, n_db))
    cos_m, sin_m = _rope_tables(jnp.arange(META_ROWS, dtype=F32) - META_LO)

    def project(x, cos, sin, with_q, lead_rows=0, n_lead=1):
        u = _norm_call(x, g_pre)
        zqk, la = _proj_call("gla", u, w_qk, (w_alow, w_a2, b_a))
        zvr = _proj_call("cast", u, w_vr, tn_pref=1024)
        kf, kb = _proj_call("k", u, w_k, (cos, sin), tm_pref=512, tn_pref=w_k.shape[1],
                            lead_rows=lead_rows, n_lead=n_lead)
        vf, vb = _proj_call("v", u, w_v, lead_rows=lead_rows, n_lead=n_lead)
        if not with_q:
            return zqk, zvr, la, kf, kb, vf, vb
        q = _proj_call("q", u, w_q, (cos, sin))
        gates = _proj_call("gate", u, w_gate, tn_pref=1024)
        return zqk, zvr, la, kf, kb, vf, vb, q, gates

    zqk_m, zvr_m, la_m, kf_m, kb_m, vf_m, vb_m = project(xm, cos_m, sin_m, False)
    zero_state = jnp.zeros((1, GLA_HEADS, GLA_DK, GLA_DV), F32)
    _, s_meta = _gla_call(zqk_m, zvr_m, la_m, zero_state, gla_norm, 1)

    zqk, zvr, la, kf_p, kb_p, vf_p, vb_p, q_p, gates_p = project(xp, cos_p, sin_p, True, N_META, n_b)
    a_p, s_p = _gla_call(zqk, zvr, la, s_meta, gla_norm, n_b)
    key_pad = ((0, LANE - META_ROWS), (0, 0))
    d_p = _attn_prompt_call(q_p, kb_p, vb_p, jnp.pad(kb_m, key_pad), jnp.pad(vb_m, key_pad), lams, diff_norm,
                            n_b, lam_init)
    h_p, u_p = _mixout_call(_mix_call(a_p, d_p, gates_p, wa, wb), wo, xp, norm_mix_post, norm_ffn_pre)
    y_p = _ffn_call(h_p, u_p, w1, w2, norm_ffn_post)

    zqk, zvr, la, kf_s, kb_s, vf_s, vb_s, q_s, gates_s = project(xs, cos_s, sin_s, True)
    a_s, s_s = _gla_call(zqk, zvr, la, state_gla[0].astype(F32), gla_norm, n_db)
    kc = cache_k[0].reshape(n_db, past * DIFF_HEADS * 2, DIFF_DQK)
    vc = cache_v[0].reshape(n_db, past * DIFF_HEADS, DIFF_DV)
    d_s = _attn_sample_call(q_s, kc, vc, kb_s, vb_s, lams, diff_norm, past, lam_init)
    h_s, u_s = _mixout_call(_mix_call(a_s, d_s, gates_s, wa, wb), wo, xs, norm_mix_post, norm_ffn_pre)
    y_s = _ffn_call(h_s, u_s, w1, w2, norm_ffn_post)

    def with_meta(f_meta, f_real):
        per_frame = f_meta.shape[0] // META_ROWS
        m = f_meta[None, META_LO * per_frame:]
        return f_real.reshape(n_b, -1, f_real.shape[1]).at[:, :N_META * per_frame].set(
            jnp.broadcast_to(m, (n_b,) + m.shape[1:]))

    k_shape = (DIFF_HEADS, 2, DIFF_DQK)
    v_shape = (DIFF_HEADS, DIFF_DV)
    return (y_p.reshape(n_b, seq, dm),
            y_s.reshape(n_db, dec_seq, dm),
            with_meta(kf_m, kf_p).reshape((1, n_b, N_META + seq) + k_shape),
            with_meta(vf_m, vf_p).reshape((1, n_b, N_META + seq) + v_shape),
            s_p[None],
            kf_s.reshape((1, n_db, dec_seq) + k_shape),
            vf_s.reshape((1, n_db, dec_seq) + v_shape),
            s_s[None])
```

```python
import functools
import math

import jax
import jax.numpy as jnp
from jax import lax
from jax.experimental import pallas as pl
from jax.experimental.pallas import tpu as pltpu

F32 = jnp.float32
BF16 = jnp.bfloat16

EPS = 1e-6
CHUNK = 64
SUB = 16
N_SUB = CHUNK // SUB
N_META = 16
META_ROWS = CHUNK
META_LO = META_ROWS - N_META
GLA_HEADS = 4
GLA_DK = 256
GLA_DV = 512
GATE_RANK = 16
GLA_TAU = 16.0
DIFF_HEADS = 8
DIFF_DQK = 128
DIFF_DV = 256
ROPE_DIM = 32
ROPE_THETA = 500000.0
LANE = 128
QUERY_SCALE = DIFF_DQK ** -0.5 * math.log2(math.e)
EPILOGUE_ROWS = 128
MLP_ROWS = 256
MASKED = -1e30
VMEM_LIMIT = 58 * 1024 * 1024

_NT = (((1,), (1,)), ((), ()))
_TN = (((0,), (0,)), ((), ()))


def _tile(n, pref):
    return pref if n % pref == 0 else n


def _params(sem):
    return pltpu.CompilerParams(dimension_semantics=sem, vmem_limit_bytes=VMEM_LIMIT)


def _rms(x):
    return x * lax.rsqrt(jnp.mean(x * x, axis=-1, keepdims=True) + EPS)


def _rope(z, cos, sin):
    half = ROPE_DIM // 2
    lane = lax.broadcasted_iota(jnp.int32, (z.shape[0], LANE), 1)
    out = []
    for g in range(z.shape[1] // LANE):
        seg = z[:, g * LANE:(g + 1) * LANE]
        partner = jnp.where(lane < half, pltpu.roll(seg, LANE - half, 1), pltpu.roll(seg, half, 1))
        out.append(seg * cos + partner * sin)
    return jnp.concatenate(out, axis=1)


def _norm_kernel(x_ref, g_ref, u_ref):
    u_ref[...] = (_rms(x_ref[...]) * g_ref[...]).astype(BF16)


def _norm_call(x, g, tm_pref=512):
    rows, d = x.shape
    tm = _tile(rows, tm_pref)
    return pl.pallas_call(
        _norm_kernel,
        grid=(rows // tm,),
        in_specs=[pl.BlockSpec((tm, d), lambda i: (i, 0)), pl.BlockSpec((1, d), lambda i: (0, 0))],
        out_specs=pl.BlockSpec((tm, d), lambda i: (i, 0)),
        out_shape=jax.ShapeDtypeStruct((rows, d), BF16),
        compiler_params=_params(("parallel",)),
        name="norm",
    )(x, g)


def _proj_kernel(*refs, mode):
    if mode == "gla":
        u_ref, w_ref, wal_ref, wa2_ref, ba_ref, z_ref, la_ref = refs
    elif mode == "q":
        u_ref, w_ref, cos_ref, sin_ref, ob_ref = refs
    elif mode == "k":
        u_ref, w_ref, cos_ref, sin_ref, of_ref, ob_ref = refs
    elif mode == "v":
        u_ref, w_ref, of_ref, ob_ref = refs
    else:
        u_ref, w_ref, ob_ref = refs

    if mode == "gla":
        @pl.when(pl.program_id(1) == 0)
        def _():
            alow = jnp.dot(u_ref[...], wal_ref[...], preferred_element_type=F32)
            pre = jnp.dot(alow.astype(BF16), wa2_ref[...], preferred_element_type=F32) + ba_ref[...]
            la_ref[...] = (jnp.minimum(pre, 0.0) - jnp.log1p(jnp.exp(-jnp.abs(pre)))) * (1.0 / GLA_TAU)

    rr = min(EPILOGUE_ROWS, u_ref.shape[0])
    for r in range(u_ref.shape[0] // rr):
        rows = slice(r * rr, (r + 1) * rr)
        z = jnp.dot(u_ref[rows, :], w_ref[...], preferred_element_type=F32)
        if mode == "gla":
            z_ref[rows, :] = z
        elif mode == "q":
            ob_ref[rows, :] = (_rope(z, cos_ref[rows, :], sin_ref[rows, :]) * QUERY_SCALE).astype(BF16)
        elif mode == "k":
            zr = _rope(z, cos_ref[rows, :], sin_ref[rows, :])
            ob_ref[rows, :] = zr.astype(BF16)
            n_g = zr.shape[1] // LANE
            for g in range(n_g):
                of_ref[pl.ds(r * rr * n_g + g, rr, stride=n_g), :] = zr[:, g * LANE:(g + 1) * LANE]
        elif mode == "v":
            of_ref[rows, :] = z
            ob_ref[rows, :] = z.astype(BF16)
        elif mode == "gate":
            ob_ref[rows, :] = (1.0 / (1.0 + jnp.exp(-z))).astype(BF16)
        else:
            ob_ref[rows, :] = z.astype(BF16)


def _proj_call(mode, u, w, extra=(), tm_pref=2048, tn_pref=512, lead_rows=0, n_lead=1):
    rows, d = u.shape
    n = w.shape[1]
    tm, tn = _tile(rows // n_lead, tm_pref), _tile(n, tn_pref)
    tiles_g = rows // n_lead // tm
    row_blk = lambda i, j: (i, 0)
    fixed = lambda i, j: (0, 0)

    def lead_row(i, width):
        return pl.multiple_of(((i // tiles_g) * (tiles_g * tm + lead_rows) + lead_rows + (i % tiles_g) * tm) * width, 8)

    in_specs = [pl.BlockSpec((tm, d), row_blk), pl.BlockSpec((d, tn), lambda i, j: (0, j))]
    out_blk = pl.BlockSpec((tm, tn), lambda i, j: (i, j))
    if mode == "gla":
        wal, wa2, ba = extra
        in_specs += [pl.BlockSpec(wal.shape, fixed), pl.BlockSpec(wa2.shape, fixed), pl.BlockSpec(ba.shape, fixed)]
        out_shape = (jax.ShapeDtypeStruct((rows, n), F32), jax.ShapeDtypeStruct((rows, wa2.shape[1]), F32))
        out_specs = (out_blk, pl.BlockSpec((tm, wa2.shape[1]), row_blk))
    elif mode in ("q", "k"):
        table_blk = lambda i, j: (i % (extra[0].shape[0] // tm), 0)
        in_specs += [pl.BlockSpec((tm, LANE), table_blk), pl.BlockSpec((tm, LANE), table_blk)]
        if mode == "q":
            out_shape, out_specs = jax.ShapeDtypeStruct((rows, n), BF16), out_blk
        else:
            assert tn == n, "native-order key rows need every (head, map) group of a frame in one block"
            n_g = n // LANE
            out_shape = (jax.ShapeDtypeStruct(((rows + n_lead * lead_rows) * n_g, LANE), F32),
                         jax.ShapeDtypeStruct((rows, n), BF16))
            out_specs = (pl.BlockSpec((pl.Element(tm * n_g), pl.Element(LANE)), lambda i, j: (lead_row(i, n_g), 0)),
                         out_blk)
    elif mode == "v":
        out_shape = (jax.ShapeDtypeStruct((rows + n_lead * lead_rows, n), F32), jax.ShapeDtypeStruct((rows, n), BF16))
        out_specs = (pl.BlockSpec((pl.Element(tm), pl.Element(tn)),
                                  lambda i, j: (lead_row(i, 1), pl.multiple_of(j * tn, LANE))), out_blk)
    else:
        out_shape, out_specs = jax.ShapeDtypeStruct((rows, n), BF16), out_blk
    return pl.pallas_call(
        functools.partial(_proj_kernel, mode=mode),
        grid=(rows // tm, n // tn),
        in_specs=in_specs,
        out_specs=out_specs,
        out_shape=out_shape,
        compiler_params=_params(("parallel", "arbitrary")),
        name="proj_" + mode,
    )(u, w, *extra)


def _gla_kernel(zqk_ref, zvr_ref, la_ref, s0_ref, gn_ref, a_ref, sout_ref, st_ref, *, n_chunks):
    c = pl.program_id(1)

    @pl.when(c == 0)
    def _():
        for h in range(GLA_HEADS):
            st_ref[h] = s0_ref[h].T

    rsub = lax.broadcasted_iota(jnp.int32, (CHUNK, GLA_DK), 0) % SUB
    rr = lax.broadcasted_iota(jnp.int32, (CHUNK, N_SUB * CHUNK), 0)
    cc = lax.broadcasted_iota(jnp.int32, (CHUNK, N_SUB * CHUNK), 1)
    valid = jnp.where(cc // CHUNK == rr // SUB, cc % CHUNK, CHUNK) <= rr
    gn = gn_ref[...]
    zero = jnp.zeros((SUB, GLA_DK), F32)
    k_off = GLA_HEADS * GLA_DK
    r_off = GLA_HEADS * GLA_DV

    def prod(xs):
        out = xs[0]
        for x in xs[1:]:
            out = out * x
        return out

    def head_chunk(h, rows):
        q = zqk_ref[rows, h * GLA_DK:(h + 1) * GLA_DK] * (GLA_DK ** -0.5)
        k = zqk_ref[rows, k_off + h * GLA_DK:k_off + (h + 1) * GLA_DK]
        vb = zvr_ref[rows, h * GLA_DV:(h + 1) * GLA_DV]
        bl = la_ref[rows, h * GLA_DK:(h + 1) * GLA_DK]
        for sh in (1, 2, 4, 8):
            bl = bl + jnp.where(rsub >= sh, pltpu.roll(bl, sh, 0), 0.0)
        et = [jnp.exp(bl[SUB * j + SUB - 1:SUB * (j + 1), :]) for j in range(N_SUB)]
        qt = q * jnp.exp(bl)
        kd = k * jnp.exp(-bl)
        qs = [qt[SUB * j:SUB * (j + 1)] for j in range(N_SUB)]
        kds = [kd[SUB * j:SUB * (j + 1)] for j in range(N_SUB)]
        khat = [kds[j] * et[j] for j in range(N_SUB)]

        blocks = []
        for i in range(N_SUB):
            for j in range(N_SUB):
                if j < i:
                    blocks.append(khat[j] if j + 1 == i else khat[j] * prod(et[j + 1:i]))
                else:
                    blocks.append(kds[j] if j == i else zero)
        kstack = jnp.concatenate(blocks, axis=0).astype(BF16)
        sc = lax.dot_general(qt.astype(BF16), kstack, _NT, preferred_element_type=F32)
        p = jnp.where(valid, sc, 0.0).astype(BF16)
        intra = jnp.dot(p, jnp.concatenate([vb] * N_SUB, axis=0), preferred_element_type=F32)

        qd = jnp.concatenate([qs[i] if i == 0 else qs[i] * prod(et[:i]) for i in range(N_SUB)], axis=0)
        st = st_ref[h]
        inter = lax.dot_general(qd.astype(BF16), st.astype(BF16), _NT, preferred_element_type=F32)
        o = inter + intra

        kdec = jnp.concatenate(
            [khat[j] if j == N_SUB - 1 else khat[j] * prod(et[j + 1:]) for j in range(N_SUB)], axis=0)
        st_ref[h] = st * prod(et) + lax.dot_general(vb, kdec.astype(BF16), _TN, preferred_element_type=F32)

        r = zvr_ref[rows, r_off + h * GLA_DV:r_off + (h + 1) * GLA_DV].astype(F32)
        a_ref[rows, h * GLA_DV:(h + 1) * GLA_DV] = (_rms(o) * gn * (r / (1.0 + jnp.exp(-r)))).astype(BF16)

    def chunk(ci, carry):
        rows = pl.ds(pl.multiple_of(ci * CHUNK, CHUNK), CHUNK)
        for h in range(GLA_HEADS):
            head_chunk(h, rows)
        return carry

    lax.fori_loop(0, n_chunks, chunk, 0, unroll=True)

    @pl.when(c == pl.num_programs(1) - 1)
    def _():
        for h in range(GLA_HEADS):
            sout_ref[h] = st_ref[h].T


def _gla_call(zqk, zvr, la, s0, gn, n_batch, tc_pref=256):
    rows = zqk.shape[0]
    per_b = rows // n_batch
    tc = _tile(per_b, tc_pref)
    n_tc = per_b // tc
    s_stride = 0 if s0.shape[0] == 1 else 1
    row_blk = lambda b, c: (b * n_tc + c, 0)
    state_blk = (None, GLA_HEADS, GLA_DK, GLA_DV)
    return pl.pallas_call(
        functools.partial(_gla_kernel, n_chunks=tc // CHUNK),
        grid=(n_batch, n_tc),
        in_specs=[
            pl.BlockSpec((tc, zqk.shape[1]), row_blk),
            pl.BlockSpec((tc, zvr.shape[1]), row_blk),
            pl.BlockSpec((tc, la.shape[1]), row_blk),
            pl.BlockSpec(state_blk, lambda b, c: (b * s_stride, 0, 0, 0)),
            pl.BlockSpec((1, GLA_DV), lambda b, c: (0, 0)),
        ],
        out_specs=(
            pl.BlockSpec((tc, GLA_HEADS * GLA_DV), row_blk),
            pl.BlockSpec(state_blk, lambda b, c: (b, 0, 0, 0)),
        ),
        out_shape=(
            jax.ShapeDtypeStruct((rows, GLA_HEADS * GLA_DV), BF16),
            jax.ShapeDtypeStruct((n_batch, GLA_HEADS, GLA_DK, GLA_DV), F32),
        ),
        scratch_shapes=[pltpu.VMEM((GLA_HEADS, GLA_DV, GLA_DK), F32)],
        compiler_params=_params(("parallel", "arbitrary")),
        name="gla",
    )(zqk, zvr, la, s0, gn)


def _lam(lq1, lk1, lq2, lk2, lam_init):
    return (jnp.exp(jnp.sum(lq1[...] * lk1[...], axis=1, keepdims=True))
            - jnp.exp(jnp.sum(lq2[...] * lk2[...], axis=1, keepdims=True)) + lam_init)


def _scores(q, k):
    s0 = lax.dot_general(q[:, :DIFF_DQK], k[:, :DIFF_DQK], _NT, preferred_element_type=F32)
    s1 = lax.dot_general(q[:, DIFF_DQK:], k[:, DIFF_DQK:], _NT, preferred_element_type=F32)
    return jnp.concatenate([s0, s1], axis=0)


def _diff_out(acc0, acc1, l0, l1, lam, dn, lam_init):
    o = acc0 * (1.0 / l0) - lam * (acc1 * (1.0 / l1))
    return (_rms(o) * dn * (1.0 - lam_init)).astype(BF16)


def _attn_prompt_kernel(q_ref, k_ref, v_ref, kp_ref, vp_ref, lq1, lk1, lq2, lk2, dn_ref, o_ref,
                        m_ref, l_ref, acc_ref, *, tq, tk, rs, lam_init):
    qi = pl.program_id(2)

    def tile_update(mp, r, kblk, vblk, mask=None, first=False):
        rows = slice(r * rs, (r + 1) * rs)
        cols = slice(mp * DIFF_DQK, (mp + 1) * DIFF_DQK)
        s = lax.dot_general(q_ref[rows, cols], kblk[:, cols], _NT, preferred_element_type=F32)
        if mask is not None:
            s = jnp.where(mask, s, MASKED)
        parts = [s[:, c * LANE:(c + 1) * LANE] for c in range(s.shape[1] // LANE)]
        m_new = jnp.broadcast_to(jnp.max(functools.reduce(jnp.maximum, parts), axis=1, keepdims=True), (rs, LANE))
        if not first:
            m_prev = m_ref[mp, rows, :]
            m_new = jnp.maximum(m_prev, m_new)
            alpha = jnp.exp2(m_prev - m_new)
        ps = [jnp.exp2(x - m_new) for x in parts]
        l_new = functools.reduce(jnp.add, ps)
        pv = jnp.dot(jnp.concatenate(ps, axis=1).astype(BF16), vblk, preferred_element_type=F32)
        if not first:
            l_new = alpha * l_ref[mp, rows, :] + l_new
            pv = jnp.concatenate([alpha] * (DIFF_DV // LANE), axis=1) * acc_ref[mp, rows, :] + pv
        l_ref[mp, rows, :] = l_new
        acc_ref[mp, rows, :] = pv
        m_ref[mp, rows, :] = m_new

    n_r = tq // rs
    pcol = lax.broadcasted_iota(jnp.int32, (rs, kp_ref.shape[0]), 1)
    pmask = (pcol >= META_LO) & (pcol < META_ROWS)
    for mp in range(2):
        for r in range(n_r):
            tile_update(mp, r, kp_ref[...], vp_ref[...], pmask, first=True)

    def full_block(kb, carry):
        krows = pl.ds(pl.multiple_of(kb * tk, tk), tk)
        kblk, vblk = k_ref[krows, :], v_ref[krows, :]
        for mp in range(2):
            for r in range(n_r):
                tile_update(mp, r, kblk, vblk)
        return carry

    lax.fori_loop(0, qi * (tq // tk), full_block, 0)

    cmask = (lax.broadcasted_iota(jnp.int32, (rs, rs), 1) // CHUNK
             <= lax.broadcasted_iota(jnp.int32, (rs, rs), 0) // CHUNK)
    for r in range(n_r):
        for d in range(r + 1):
            krows = pl.ds(pl.multiple_of(qi * tq + d * rs, rs), rs)
            kblk, vblk = k_ref[krows, :], v_ref[krows, :]
            for mp in range(2):
                tile_update(mp, r, kblk, vblk, cmask if d == r else None)

    lam = _lam(lq1, lk1, lq2, lk2, lam_init)
    l = jnp.sum(l_ref[...], axis=2, keepdims=True)
    o_ref[...] = _diff_out(acc_ref[0], acc_ref[1], l[0], l[1], lam, dn_ref[...], lam_init)


def _attn_prompt_call(q, kb, vb, kmeta, vmeta, lams, dn, n_batch, lam_init, tq_pref=2048, tk_pref=512, rs_pref=256):
    rows, width = q.shape
    t = rows // n_batch
    tq = _tile(t, tq_pref)
    tk = _tile(tq, tk_pref)
    rs = _tile(tq, rs_pref)
    nq = t // tq
    hw = 2 * DIFF_DQK
    fixed = lambda b, h, i: (0, 0)
    kv_spec = pl.BlockSpec((t, hw), lambda b, h, i: (b, h))
    meta_spec = pl.BlockSpec((kmeta.shape[0], hw), lambda b, h, i: (0, h))
    vec = pl.BlockSpec((1, DIFF_DQK), fixed)
    return pl.pallas_call(
        functools.partial(_attn_prompt_kernel, tq=tq, tk=tk, rs=rs, lam_init=lam_init),
        grid=(n_batch, DIFF_HEADS, nq),
        in_specs=[pl.BlockSpec((tq, hw), lambda b, h, i: (b * nq + i, h)), kv_spec, kv_spec, meta_spec, meta_spec,
                  vec, vec, vec, vec, pl.BlockSpec((1, DIFF_DV), fixed)],
        out_specs=pl.BlockSpec((tq, DIFF_DV), lambda b, h, i: (b * nq + i, h)),
        out_shape=jax.ShapeDtypeStruct((rows, width), BF16),
        scratch_shapes=[pltpu.VMEM((2, tq, LANE), F32), pltpu.VMEM((2, tq, LANE), F32), pltpu.VMEM((2, tq, DIFF_DV), F32)],
        compiler_params=_params(("parallel", "parallel", "arbitrary")),
        name="attn_prompt",
    )(q, kb, vb, kmeta, vmeta, *lams, dn)


def _attn_sample_kernel(q_ref, kc_ref, vlo_ref, vhi_ref, kn_ref, vn_ref, lq1, lk1, lq2, lk2, dn_ref, o_ref, *,
                        past, lam_init):
    lam = _lam(lq1, lk1, lq2, lk2, lam_init)
    dn = dn_ref[...]
    hw = 2 * DIFF_DQK
    for h in range(DIFF_HEADS):
        cols = slice(h * hw, (h + 1) * hw)
        q = q_ref[:, cols]
        tq = q.shape[0]
        kc = jnp.concatenate(
            [kc_ref[pl.ds(2 * h + mp, past, stride=2 * DIFF_HEADS), :] for mp in range(2)], axis=1).astype(BF16)
        vc = jnp.concatenate(
            [ref[pl.ds(h, past, stride=DIFF_HEADS), :] for ref in (vlo_ref, vhi_ref)], axis=1).astype(BF16)
        sc = _scores(q, kc)
        sn = _scores(q, kn_ref[:, cols])
        m = jnp.maximum(jnp.max(sc, axis=1, keepdims=True), jnp.max(sn, axis=1, keepdims=True))
        pc = jnp.exp2(sc - m)
        pn = jnp.exp2(sn - m)
        l = jnp.sum(pc, axis=1, keepdims=True) + jnp.sum(pn, axis=1, keepdims=True)
        acc = (jnp.dot(pc.astype(BF16), vc, preferred_element_type=F32)
               + jnp.dot(pn.astype(BF16), vn_ref[:, cols], preferred_element_type=F32))
        o_ref[:, cols] = _diff_out(acc[:tq], acc[tq:], l[:tq], l[tq:], lam, dn, lam_init)


def _attn_sample_call(q, kcache, vcache, knew, vnew, lams, dn, past, lam_init):
    rows, width = q.shape
    n_batch = kcache.shape[0]
    t = rows // n_batch
    fixed = lambda b: (0, 0)
    new_spec = pl.BlockSpec((t, width), lambda b: (b, 0))
    vec = pl.BlockSpec((1, DIFF_DQK), fixed)
    return pl.pallas_call(
        functools.partial(_attn_sample_kernel, past=past, lam_init=lam_init),
        grid=(n_batch,),
        in_specs=[new_spec, pl.BlockSpec((None,) + kcache.shape[1:], lambda b: (b, 0, 0)),
                  pl.BlockSpec((None, vcache.shape[1], LANE), lambda b: (b, 0, 0)),
                  pl.BlockSpec((None, vcache.shape[1], LANE), lambda b: (b, 0, 1)), new_spec, new_spec,
                  vec, vec, vec, vec, pl.BlockSpec((1, DIFF_DV), fixed)],
        out_specs=new_spec,
        out_shape=jax.ShapeDtypeStruct((rows, width), BF16),
        compiler_params=_params(("parallel",)),
        name="attn_sample",
    )(q, kcache, vcache, vcache, knew, vnew, *lams, dn)


def _mix_kernel(a_ref, d_ref, ga_ref, gb_ref, wa_ref, wb_ref, o_ref):
    rr = min(MLP_ROWS, o_ref.shape[0])
    for r in range(o_ref.shape[0] // rr):
        rows = slice(r * rr, (r + 1) * rr)
        mix = (ga_ref[rows, :] * jnp.dot(a_ref[rows, :], wa_ref[...], preferred_element_type=F32)
               + gb_ref[rows, :] * jnp.dot(d_ref[rows, :], wb_ref[...], preferred_element_type=F32))
        o_ref[rows, :] = mix.astype(BF16)


def _mix_call(a, d, gates, wa, wb, tm_pref=1024, tn_pref=512):
    rows, k = a.shape
    n = wa.shape[1]
    tm, tn = _tile(rows, tm_pref), _tile(n, tn_pref)
    nj = n // tn
    row_blk = lambda i, j: (i, 0)
    return pl.pallas_call(
        _mix_kernel,
        grid=(rows // tm, nj),
        in_specs=[pl.BlockSpec((tm, k), row_blk), pl.BlockSpec((tm, k), row_blk),
                  pl.BlockSpec((tm, tn), lambda i, j: (i, j)), pl.BlockSpec((tm, tn), lambda i, j: (i, nj + j)),
                  pl.BlockSpec((k, tn), lambda i, j: (0, j)), pl.BlockSpec((k, tn), lambda i, j: (0, j))],
        out_specs=pl.BlockSpec((tm, tn), lambda i, j: (i, j)),
        out_shape=jax.ShapeDtypeStruct((rows, n), BF16),
        compiler_params=_params(("parallel", "arbitrary")),
        name="mix",
    )(a, d, gates, gates, wa, wb)


def _mixout_kernel(mix_ref, wo_ref, res_ref, gpost_ref, gnext_ref, h_ref, u_ref):
    rr = min(EPILOGUE_ROWS, h_ref.shape[0])
    for r in range(h_ref.shape[0] // rr):
        rows = slice(r * rr, (r + 1) * rr)
        t = jnp.dot(mix_ref[rows, :], wo_ref[...], preferred_element_type=F32)
        h = res_ref[rows, :] + _rms(t) * gpost_ref[...]
        h_ref[rows, :] = h
        u_ref[rows, :] = (_rms(h) * gnext_ref[...]).astype(BF16)


def _mixout_call(mix, wo, res, gpost, gnext, tm_pref=512):
    rows, dm = res.shape
    tm = _tile(rows, tm_pref)
    row_blk = lambda i: (i, 0)
    vec = pl.BlockSpec((1, dm), lambda i: (0, 0))
    return pl.pallas_call(
        _mixout_kernel,
        grid=(rows // tm,),
        in_specs=[pl.BlockSpec((tm, mix.shape[1]), row_blk), pl.BlockSpec(wo.shape, lambda i: (0, 0)),
                  pl.BlockSpec((tm, dm), row_blk), vec, vec],
        out_specs=(pl.BlockSpec((tm, dm), row_blk), pl.BlockSpec((tm, dm), row_blk)),
        out_shape=(jax.ShapeDtypeStruct((rows, dm), F32), jax.ShapeDtypeStruct((rows, dm), BF16)),
        compiler_params=_params(("parallel",)),
        name="mixout",
    )(mix, wo, res, gpost, gnext)


def _ffn_kernel(res_ref, u_ref, w1_ref, w2_ref, gpost_ref, o_ref):
    j = pl.program_id(1)

    @pl.when(j == 0)
    def _():
        o_ref[...] = jnp.zeros(o_ref.shape, F32)

    rr = min(MLP_ROWS, o_ref.shape[0])
    for r in range(o_ref.shape[0] // rr):
        rows = slice(r * rr, (r + 1) * rr)
        mid = jnp.square(jnp.maximum(jnp.dot(u_ref[rows, :], w1_ref[...], preferred_element_type=F32), 0.0))
        o_ref[rows, :] += jnp.dot(mid.astype(BF16), w2_ref[...], preferred_element_type=F32)

    @pl.when(j == pl.num_programs(1) - 1)
    def _():
        o_ref[...] = res_ref[...] + _rms(o_ref[...]) * gpost_ref[...]


def _ffn_call(h, u, w1, w2, gpost, tm_pref=512, tf_pref=2048):
    rows, dm = h.shape
    dff = w1.shape[1]
    tm, tf = _tile(rows, tm_pref), _tile(dff, tf_pref)
    row_blk = lambda i, j: (i, 0)
    vec = pl.BlockSpec((1, dm), lambda i, j: (0, 0))
    return pl.pallas_call(
        _ffn_kernel,
        grid=(rows // tm, dff // tf),
        in_specs=[pl.BlockSpec((tm, dm), row_blk), pl.BlockSpec((tm, dm), row_blk),
                  pl.BlockSpec((dm, tf), lambda i, j: (0, j)), pl.BlockSpec((tf, dm), lambda i, j: (j, 0)), vec],
        out_specs=pl.BlockSpec((tm, dm), row_blk),
        out_shape=jax.ShapeDtypeStruct((rows, dm), F32),
        compiler_params=_params(("parallel", "arbitrary")),
        name="ffn",
    )(h, u, w1, w2, gpost)


def _rope_tables(pos):
    inv_freq = jnp.power(ROPE_THETA, -jnp.arange(0, ROPE_DIM, 2, dtype=F32) / ROPE_DIM)
    ang = pos[:, None] * inv_freq[None, :]
    cos, sin = jnp.cos(ang), jnp.sin(ang)
    n = pos.shape[0]
    pad = LANE - ROPE_DIM
    return (jnp.concatenate([cos, cos, jnp.ones((n, pad), F32)], axis=1),
            jnp.concatenate([-sin, sin, jnp.zeros((n, pad), F32)], axis=1))


def kernel(x_prompt, x_sample, cache_k, cache_v, state_gla, meta, norm_mix_pre, w_in, w_gla_a2, b_gla_a, gla_norm, diff_lq1, diff_lk1, diff_lq2, diff_lk2, diff_norm, w_br_gla, w_br_diff, w_o, norm_mix_post, norm_ffn_pre, w_ff1, w_ff2, norm_ffn_post):
    n_b, seq, dm = x_prompt.shape
    n_db, dec_seq, _ = x_sample.shape
    past = cache_k.shape[2]
    assert w_in.shape[0] == 1, "single-layer step only"
    assert dec_seq == CHUNK and seq % CHUNK == 0 and meta.shape[0] == N_META
    lam_init = 0.8 - 0.6 * math.exp(-0.3 * 0)

    sizes = (GLA_HEADS * GLA_DK, GLA_HEADS * GLA_DK, GLA_HEADS * GLA_DV, GATE_RANK, GLA_HEADS * GLA_DV,
             2 * DIFF_HEADS * DIFF_DQK, 2 * DIFF_HEADS * DIFF_DQK, DIFF_HEADS * DIFF_DV, dm, dm)
    offs = [0]
    for s in sizes:
        offs.append(offs[-1] + s)
    wi = w_in[0]
    col = lambda i: wi[:, offs[i]:offs[i + 1]]
    w_qk = jnp.concatenate([col(0), col(1)], axis=1).astype(BF16)
    w_vr = jnp.concatenate([col(2), col(4)], axis=1).astype(BF16)
    w_alow = jnp.pad(col(3), ((0, 0), (0, LANE - GATE_RANK))).astype(BF16)
    w_a2 = jnp.pad(w_gla_a2[0], ((0, LANE - GATE_RANK), (0, 0))).astype(BF16)
    w_q, w_k, w_v = col(5).astype(BF16), col(6).astype(BF16), col(7).astype(BF16)
    w_gate = jnp.concatenate([col(8), col(9)], axis=1).astype(BF16)
    b_a = b_gla_a
    g_pre = norm_mix_pre
    wa, wb, wo = w_br_gla[0].astype(BF16), w_br_diff[0].astype(BF16), w_o[0].astype(BF16)
    w1, w2 = w_ff1[0].astype(BF16), w_ff2[0].astype(BF16)
    lams = (diff_lq1, diff_lk1, diff_lq2, diff_lk2)

    xp = x_prompt.reshape(n_b * seq, dm)
    xs = x_sample.reshape(n_db * dec_seq, dm)
    xm = jnp.concatenate([jnp.zeros((META_LO, dm), x_prompt.dtype), meta.astype(x_prompt.dtype)], axis=0)
    cos_p, sin_p = _rope_tables(jnp.arange(seq, dtype=F32) + N_META)
    cos_s, sin_s = _rope_tables(jnp.tile(jnp.arange(dec_seq, dtype=F32) + past, n_db))
    cos_m, sin_m = _rope_tables(jnp.arange(META_ROWS, dtype=F32) - META_LO)

    def project(x, cos, sin, with_q, lead_rows=0, n_lead=1):
        u = _norm_call(x, g_pre)
        zqk, la = _proj_call("gla", u, w_qk, (w_alow, w_a2, b_a))
        zvr = _proj_call("cast", u, w_vr, tn_pref=1024)
        kf, kb = _proj_call("k", u, w_k, (cos, sin), tm_pref=512, tn_pref=w_k.shape[1],
                            lead_rows=lead_rows, n_lead=n_lead)
        vf, vb = _proj_call("v", u, w_v, lead_rows=lead_rows, n_lead=n_lead)
        if not with_q:
            return zqk, zvr, la, kf, kb, vf, vb
        q = _proj_call("q", u, w_q, (cos, sin))
        gates = _proj_call("gate", u, w_gate, tn_pref=1024)
        return zqk, zvr, la, kf, kb, vf, vb, q, gates

    zqk_m, zvr_m, la_m, kf_m, kb_m, vf_m, vb_m = project(xm, cos_m, sin_m, False)
    zero_state = jnp.zeros((1, GLA_HEADS, GLA_DK, GLA_DV), F32)
    _, s_meta = _gla_call(zqk_m, zvr_m, la_m, zero_state, gla_norm, 1)

    zqk, zvr, la, kf_p, kb_p, vf_p, vb_p, q_p, gates_p = project(xp, cos_p, sin_p, True, N_META, n_b)
    a_p, s_p = _gla_call(zqk, zvr, la, s_meta, gla_norm, n_b)
    key_pad = ((0, LANE - META_ROWS), (0, 0))
    d_p = _attn_prompt_call(q_p, kb_p, vb_p, jnp.pad(kb_m, key_pad), jnp.pad(vb_m, key_pad), lams, diff_norm,
                            n_b, lam_init)
    h_p, u_p = _mixout_call(_mix_call(a_p, d_p, gates_p, wa, wb), wo, xp, norm_mix_post, norm_ffn_pre)
    y_p = _ffn_call(h_p, u_p, w1, w2, norm_ffn_post)

    zqk, zvr, la, kf_s, kb_s, vf_s, vb_s, q_s, gates_s = project(xs, cos_s, sin_s, True)
    a_s, s_s = _gla_call(zqk, zvr, la, state_gla[0].astype(F32), gla_norm, n_db)
    kc = cache_k[0].reshape(n_db, past * DIFF_HEADS * 2, DIFF_DQK)
    vc = cache_v[0].reshape(n_db, past * DIFF_HEADS, DIFF_DV)
    d_s = _attn_sample_call(q_s, kc, vc, kb_s, vb_s, lams, diff_norm, past, lam_init)
    h_s, u_s = _mixout_call(_mix_call(a_s, d_s, gates_s, wa, wb), wo, xs, norm_mix_post, norm_ffn_pre)
    y_s = _ffn_call(h_s, u_s, w1, w2, norm_ffn_post)

    def with_meta(f_meta, f_real):
        per_frame = f_meta.shape[0] // META_ROWS
        m = f_meta[None, META_LO * per_frame:]
        return f_real.reshape(n_b, -1, f_real.shape[1]).at[:, :N_META * per_frame].set(
            jnp.broadcast_to(m, (n_b,) + m.shape[1:]))

    k_shape = (DIFF_HEADS, 2, DIFF_DQK)
    v_shape = (DIFF_HEADS, DIFF_DV)
    return (y_p.reshape(n_b, seq, dm),
            y_s.reshape(n_db, dec_seq, dm),
            with_meta(kf_m, kf_p).reshape((1, n_b, N_META + seq) + k_shape),
            with_meta(vf_m, vf_p).reshape((1, n_b, N_META + seq) + v_shape),
            s_p[None],
            kf_s.reshape((1, n_db, dec_seq) + k_shape),
            vf_s.reshape((1, n_db, dec_seq) + v_shape),
            s_s[None])
```

```python
import functools
import math

import jax
import jax.numpy as jnp
from jax import lax
from jax.experimental import pallas as pl
from jax.experimental.pallas import tpu as pltpu

F32 = jnp.float32
BF16 = jnp.bfloat16

EPS = 1e-6
CHUNK = 64
SUB = 16
N_SUB = CHUNK // SUB
N_META = 16
META_ROWS = CHUNK
META_LO = META_ROWS - N_META
GLA_HEADS = 4
GLA_DK = 256
GLA_DV = 512
GATE_RANK = 16
GLA_TAU = 16.0
DIFF_HEADS = 8
DIFF_DQK = 128
DIFF_DV = 256
ROPE_DIM = 32
ROPE_THETA = 500000.0
LANE = 128
QUERY_SCALE = DIFF_DQK ** -0.5 * math.log2(math.e)
EPILOGUE_ROWS = 128
MLP_ROWS = 256
MASKED = -1e30
VMEM_LIMIT = 58 * 1024 * 1024

_NT = (((1,), (1,)), ((), ()))
_TN = (((0,), (0,)), ((), ()))


def _tile(n, pref):
    return pref if n % pref == 0 else n


def _params(sem):
    return pltpu.CompilerParams(dimension_semantics=sem, vmem_limit_bytes=VMEM_LIMIT)


def _rms(x):
    return x * lax.rsqrt(jnp.mean(x * x, axis=-1, keepdims=True) + EPS)


def _rope(z, cos, sin):
    half = ROPE_DIM // 2
    lane = lax.broadcasted_iota(jnp.int32, (z.shape[0], LANE), 1)
    out = []
    for g in range(z.shape[1] // LANE):
        seg = z[:, g * LANE:(g + 1) * LANE]
        partner = jnp.where(lane < half, pltpu.roll(seg, LANE - half, 1), pltpu.roll(seg, half, 1))
        out.append(seg * cos + partner * sin)
    return jnp.concatenate(out, axis=1)


def _norm_kernel(x_ref, g_ref, u_ref):
    u_ref[...] = (_rms(x_ref[...]) * g_ref[...]).astype(BF16)


def _norm_call(x, g, tm_pref=512):
    rows, d = x.shape
    tm = _tile(rows, tm_pref)
    return pl.pallas_call(
        _norm_kernel,
        grid=(rows // tm,),
        in_specs=[pl.BlockSpec((tm, d), lambda i: (i, 0)), pl.BlockSpec((1, d), lambda i: (0, 0))],
        out_specs=pl.BlockSpec((tm, d), lambda i: (i, 0)),
        out_shape=jax.ShapeDtypeStruct((rows, d), BF16),
        compiler_params=_params(("parallel",)),
        name="norm",
    )(x, g)


def _proj_kernel(*refs, mode):
    if mode == "gla":
        u_ref, w_ref, wal_ref, wa2_ref, ba_ref, z_ref, la_ref = refs
    elif mode == "q":
        u_ref, w_ref, cos_ref, sin_ref, ob_ref = refs
    elif mode == "k":
        u_ref, w_ref, cos_ref, sin_ref, of_ref, ob_ref = refs
    elif mode == "v":
        u_ref, w_ref, of_ref, ob_ref = refs
    else:
        u_ref, w_ref, ob_ref = refs

    if mode == "gla":
        @pl.when(pl.program_id(1) == 0)
        def _():
            alow = jnp.dot(u_ref[...], wal_ref[...], preferred_element_type=F32)
            pre = jnp.dot(alow.astype(BF16), wa2_ref[...], preferred_element_type=F32) + ba_ref[...]
            la_ref[...] = (jnp.minimum(pre, 0.0) - jnp.log1p(jnp.exp(-jnp.abs(pre)))) * (1.0 / GLA_TAU)

    rr = min(EPILOGUE_ROWS, u_ref.shape[0])
    for r in range(u_ref.shape[0] // rr):
        rows = slice(r * rr, (r + 1) * rr)
        z = jnp.dot(u_ref[rows, :], w_ref[...], preferred_element_type=F32)
        if mode == "gla":
            z_ref[rows, :] = z
        elif mode == "q":
            ob_ref[rows, :] = (_rope(z, cos_ref[rows, :], sin_ref[rows, :]) * QUERY_SCALE).astype(BF16)
        elif mode == "k":
            zr = _rope(z, cos_ref[rows, :], sin_ref[rows, :])
            ob_ref[rows, :] = zr.astype(BF16)
            n_g = zr.shape[1] // LANE
            for g in range(n_g):
                of_ref[pl.ds(r * rr * n_g + g, rr, stride=n_g), :] = zr[:, g * LANE:(g + 1) * LANE]
        elif mode == "v":
            of_ref[rows, :] = z
            ob_ref[rows, :] = z.astype(BF16)
        elif mode == "gate":
            ob_ref[rows, :] = (1.0 / (1.0 + jnp.exp(-z))).astype(BF16)
        else:
            ob_ref[rows, :] = z.astype(BF16)


def _proj_call(mode, u, w, extra=(), tm_pref=2048, tn_pref=512, lead_rows=0, n_lead=1):
    rows, d = u.shape
    n = w.shape[1]
    tm, tn = _tile(rows // n_lead, tm_pref), _tile(n, tn_pref)
    tiles_g = rows // n_lead // tm
    row_blk = lambda i, j: (i, 0)
    fixed = lambda i, j: (0, 0)

    def lead_row(i, width):
        return pl.multiple_of(((i // tiles_g) * (tiles_g * tm + lead_rows) + lead_rows + (i % tiles_g) * tm) * width, 8)

    in_specs = [pl.BlockSpec((tm, d), row_blk), pl.BlockSpec((d, tn), lambda i, j: (0, j))]
    out_blk = pl.BlockSpec((tm, tn), lambda i, j: (i, j))
    if mode == "gla":
        wal, wa2, ba = extra
        in_specs += [pl.BlockSpec(wal.shape, fixed), pl.BlockSpec(wa2.shape, fixed), pl.BlockSpec(ba.shape, fixed)]
        out_shape = (jax.ShapeDtypeStruct((rows, n), F32), jax.ShapeDtypeStruct((rows, wa2.shape[1]), F32))
        out_specs = (out_blk, pl.BlockSpec((tm, wa2.shape[1]), row_blk))
    elif mode in ("q", "k"):
        table_blk = lambda i, j: (i % (extra[0].shape[0] // tm), 0)
        in_specs += [pl.BlockSpec((tm, LANE), table_blk), pl.BlockSpec((tm, LANE), table_blk)]
        if mode == "q":
            out_shape, out_specs = jax.ShapeDtypeStruct((rows, n), BF16), out_blk
        else:
            assert tn == n, "native-order key rows need every (head, map) group of a frame in one block"
            n_g = n // LANE
            out_shape = (jax.ShapeDtypeStruct(((rows + n_lead * lead_rows) * n_g, LANE), F32),
                         jax.ShapeDtypeStruct((rows, n), BF16))
            out_specs = (pl.BlockSpec((pl.Element(tm * n_g), pl.Element(LANE)), lambda i, j: (lead_row(i, n_g), 0)),
                         out_blk)
    elif mode == "v":
        out_shape = (jax.ShapeDtypeStruct((rows + n_lead * lead_rows, n), F32), jax.ShapeDtypeStruct((rows, n), BF16))
        out_specs = (pl.BlockSpec((pl.Element(tm), pl.Element(tn)),
                                  lambda i, j: (lead_row(i, 1), pl.multiple_of(j * tn, LANE))), out_blk)
    else:
        out_shape, out_specs = jax.ShapeDtypeStruct((rows, n), BF16), out_blk
    return pl.pallas_call(
        functools.partial(_proj_kernel, mode=mode),
        grid=(rows // tm, n // tn),
        in_specs=in_specs,
        out_specs=out_specs,
        out_shape=out_shape,
        compiler_params=_params(("parallel", "arbitrary")),
        name="proj_" + mode,
    )(u, w, *extra)


def _gla_kernel(zqk_ref, zvr_ref, la_ref, s0_ref, gn_ref, a_ref, sout_ref, st_ref, *, n_chunks):
    c = pl.program_id(1)

    @pl.when(c == 0)
    def _():
        for h in range(GLA_HEADS):
            st_ref[h] = s0_ref[h].T

    rsub = lax.broadcasted_iota(jnp.int32, (CHUNK, GLA_DK), 0) % SUB
    rr = lax.broadcasted_iota(jnp.int32, (CHUNK, N_SUB * CHUNK), 0)
    cc = lax.broadcasted_iota(jnp.int32, (CHUNK, N_SUB * CHUNK), 1)
    valid = jnp.where(cc // CHUNK == rr // SUB, cc % CHUNK, CHUNK) <= rr
    gn = gn_ref[...]
    zero = jnp.zeros((SUB, GLA_DK), F32)
    k_off = GLA_HEADS * GLA_DK
    r_off = GLA_HEADS * GLA_DV

    def prod(xs):
        out = xs[0]
        for x in xs[1:]:
            out = out * x
        return out

    def head_chunk(h, rows):
        q = zqk_ref[rows, h * GLA_DK:(h + 1) * GLA_DK] * (GLA_DK ** -0.5)
        k = zqk_ref[rows, k_off + h * GLA_DK:k_off + (h + 1) * GLA_DK]
        vb = zvr_ref[rows, h * GLA_DV:(h + 1) * GLA_DV]
        bl = la_ref[rows, h * GLA_DK:(h + 1) * GLA_DK]
        for sh in (1, 2, 4, 8):
            bl = bl + jnp.where(rsub >= sh, pltpu.roll(bl, sh, 0), 0.0)
        et = [jnp.exp(bl[SUB * j + SUB - 1:SUB * (j + 1), :]) for j in range(N_SUB)]
        qt = q * jnp.exp(bl)
        kd = k * jnp.exp(-bl)
        qs = [qt[SUB * j:SUB * (j + 1)] for j in range(N_SUB)]
        kds = [kd[SUB * j:SUB * (j + 1)] for j in range(N_SUB)]
        khat = [kds[j] * et[j] for j in range(N_SUB)]

        blocks = []
        for i in range(N_SUB):
            for j in range(N_SUB):
                if j < i:
                    blocks.append(khat[j] if j + 1 == i else khat[j] * prod(et[j + 1:i]))
                else:
                    blocks.append(kds[j] if j == i else zero)
        kstack = jnp.concatenate(blocks, axis=0).astype(BF16)
        sc = lax.dot_general(qt.astype(BF16), kstack, _NT, preferred_element_type=F32)
        p = jnp.where(valid, sc, 0.0).astype(BF16)
        intra = jnp.dot(p, jnp.concatenate([vb] * N_SUB, axis=0), preferred_element_type=F32)

        qd = jnp.concatenate([qs[i] if i == 0 else qs[i] * prod(et[:i]) for i in range(N_SUB)], axis=0)
        st = st_ref[h]
        inter = lax.dot_general(qd.astype(BF16), st.astype(BF16), _NT, preferred_element_type=F32)
        o = inter + intra

        kdec = jnp.concatenate(
            [khat[j] if j == N_SUB - 1 else khat[j] * prod(et[j + 1:]) for j in range(N_SUB)], axis=0)
        st_ref[h] = st * prod(et) + lax.dot_general(vb, kdec.astype(BF16), _TN, preferred_element_type=F32)

        r = zvr_ref[rows, r_off + h * GLA_DV:r_off + (h + 1) * GLA_DV].astype(F32)
        a_ref[rows, h * GLA_DV:(h + 1) * GLA_DV] = (_rms(o) * gn * (r / (1.0 + jnp.exp(-r)))).astype(BF16)

    def chunk(ci, carry):
        rows = pl.ds(pl.multiple_of(ci * CHUNK, CHUNK), CHUNK)
        for h in range(GLA_HEADS):
            head_chunk(h, rows)
        return carry

    lax.fori_loop(0, n_chunks, chunk, 0, unroll=True)

    @pl.when(c == pl.num_programs(1) - 1)
    def _():
        for h in range(GLA_HEADS):
            sout_ref[h] = st_ref[h].T


def _gla_call(zqk, zvr, la, s0, gn, n_batch, tc_pref=256):
    rows = zqk.shape[0]
    per_b = rows // n_batch
    tc = _tile(per_b, tc_pref)
    n_tc = per_b // tc
    s_stride = 0 if s0.shape[0] == 1 else 1
    row_blk = lambda b, c: (b * n_tc + c, 0)
    state_blk = (None, GLA_HEADS, GLA_DK, GLA_DV)
    return pl.pallas_call(
        functools.partial(_gla_kernel, n_chunks=tc // CHUNK),
        grid=(n_batch, n_tc),
        in_specs=[
            pl.BlockSpec((tc, zqk.shape[1]), row_blk),
            pl.BlockSpec((tc, zvr.shape[1]), row_blk),
            pl.BlockSpec((tc, la.shape[1]), row_blk),
            pl.BlockSpec(state_blk, lambda b, c: (b * s_stride, 0, 0, 0)),
            pl.BlockSpec((1, GLA_DV), lambda b, c: (0, 0)),
        ],
        out_specs=(
            pl.BlockSpec((tc, GLA_HEADS * GLA_DV), row_blk),
            pl.BlockSpec(state_blk, lambda b, c: (b, 0, 0, 0)),
        ),
        out_shape=(
            jax.ShapeDtypeStruct((rows, GLA_HEADS * GLA_DV), BF16),
            jax.ShapeDtypeStruct((n_batch, GLA_HEADS, GLA_DK, GLA_DV), F32),
        ),
        scratch_shapes=[pltpu.VMEM((GLA_HEADS, GLA_DV, GLA_DK), F32)],
        compiler_params=_params(("parallel", "arbitrary")),
        name="gla",
    )(zqk, zvr, la, s0, gn)


def _lam(lq1, lk1, lq2, lk2, lam_init):
    return (jnp.exp(jnp.sum(lq1[...] * lk1[...], axis=1, keepdims=True))
            - jnp.exp(jnp.sum(lq2[...] * lk2[...], axis=1, keepdims=True)) + lam_init)


def _scores(q, k):
    s0 = lax.dot_general(q[:, :DIFF_DQK], k[:, :DIFF_DQK], _NT, preferred_element_type=F32)
    s1 = lax.dot_general(q[:, DIFF_DQK:], k[:, DIFF_DQK:], _NT, preferred_element_type=F32)
    return jnp.concatenate([s0, s1], axis=0)


def _diff_out(acc0, acc1, l0, l1, lam, dn, lam_init):
    o = acc0 * (1.0 / l0) - lam * (acc1 * (1.0 / l1))
    return (_rms(o) * dn * (1.0 - lam_init)).astype(BF16)


def _attn_prompt_kernel(q_ref, k_ref, v_ref, kp_ref, vp_ref, lq1, lk1, lq2, lk2, dn_ref, o_ref,
                        m_ref, l_ref, acc_ref, *, tq, tk, rs, lam_init):
    qi = pl.program_id(2)

    def tile_update(mp, r, kblk, vblk, mask=None, first=False):
        rows = slice(r * rs, (r + 1) * rs)
        cols = slice(mp * DIFF_DQK, (mp + 1) * DIFF_DQK)
        s = lax.dot_general(q_ref[rows, cols], kblk[:, cols], _NT, preferred_element_type=F32)
        if mask is not None:
            s = jnp.where(mask, s, MASKED)
        parts = [s[:, c * LANE:(c + 1) * LANE] for c in range(s.shape[1] // LANE)]
        m_new = jnp.broadcast_to(jnp.max(functools.reduce(jnp.maximum, parts), axis=1, keepdims=True), (rs, LANE))
        if not first:
            m_prev = m_ref[mp, rows, :]
            m_new = jnp.maximum(m_prev, m_new)
            alpha = jnp.exp2(m_prev - m_new)
        ps = [jnp.exp2(x - m_new) for x in parts]
        l_new = functools.reduce(jnp.add, ps)
        pv = jnp.dot(jnp.concatenate(ps, axis=1).astype(BF16), vblk, preferred_element_type=F32)
        if not first:
            l_new = alpha * l_ref[mp, rows, :] + l_new
            pv = jnp.concatenate([alpha] * (DIFF_DV // LANE), axis=1) * acc_ref[mp, rows, :] + pv
        l_ref[mp, rows, :] = l_new
        acc_ref[mp, rows, :] = pv
        m_ref[mp, rows, :] = m_new

    n_r = tq // rs
    pcol = lax.broadcasted_iota(jnp.int32, (rs, kp_ref.shape[0]), 1)
    pmask = (pcol >= META_LO) & (pcol < META_ROWS)
    for mp in range(2):
        for r in range(n_r):
            tile_update(mp, r, kp_ref[...], vp_ref[...], pmask, first=True)

    def full_block(kb, carry):
        krows = pl.ds(pl.multiple_of(kb * tk, tk), tk)
        kblk, vblk = k_ref[krows, :], v_ref[krows, :]
        for mp in range(2):
            for r in range(n_r):
                tile_update(mp, r, kblk, vblk)
        return carry

    lax.fori_loop(0, qi * (tq // tk), full_block, 0)

    cmask = (lax.broadcasted_iota(jnp.int32, (rs, rs), 1) // CHUNK
             <= lax.broadcasted_iota(jnp.int32, (rs, rs), 0) // CHUNK)
    for r in range(n_r):
        for d in range(r + 1):
            krows = pl.ds(pl.multiple_of(qi * tq + d * rs, rs), rs)
            kblk, vblk = k_ref[krows, :], v_ref[krows, :]
            for mp in range(2):
                tile_update(mp, r, kblk, vblk, cmask if d == r else None)

    lam = _lam(lq1, lk1, lq2, lk2, lam_init)
    l = jnp.sum(l_ref[...], axis=2, keepdims=True)
    o_ref[...] = _diff_out(acc_ref[0], acc_ref[1], l[0], l[1], lam, dn_ref[...], lam_init)


def _attn_prompt_call(q, kb, vb, kmeta, vmeta, lams, dn, n_batch, lam_init, tq_pref=2048, tk_pref=512, rs_pref=256):
    rows, width = q.shape
    t = rows // n_batch
    tq = _tile(t, tq_pref)
    tk = _tile(tq, tk_pref)
    rs = _tile(tq, rs_pref)
    nq = t // tq
    hw = 2 * DIFF_DQK
    fixed = lambda b, h, i: (0, 0)
    kv_spec = pl.BlockSpec((t, hw), lambda b, h, i: (b, h))
    meta_spec = pl.BlockSpec((kmeta.shape[0], hw), lambda b, h, i: (0, h))
    vec = pl.BlockSpec((1, DIFF_DQK), fixed)
    return pl.pallas_call(
        functools.partial(_attn_prompt_kernel, tq=tq, tk=tk, rs=rs, lam_init=lam_init),
        grid=(n_batch, DIFF_HEADS, nq),
        in_specs=[pl.BlockSpec((tq, hw), lambda b, h, i: (b * nq + i, h)), kv_spec, kv_spec, meta_spec, meta_spec,
                  vec, vec, vec, vec, pl.BlockSpec((1, DIFF_DV), fixed)],
        out_specs=pl.BlockSpec((tq, DIFF_DV), lambda b, h, i: (b * nq + i, h)),
        out_shape=jax.ShapeDtypeStruct((rows, width), BF16),
        scratch_shapes=[pltpu.VMEM((2, tq, LANE), F32), pltpu.VMEM((2, tq, LANE), F32), pltpu.VMEM((2, tq, DIFF_DV), F32)],
        compiler_params=_params(("parallel", "parallel", "arbitrary")),
        name="attn_prompt",
    )(q, kb, vb, kmeta, vmeta, *lams, dn)


def _attn_sample_kernel(q_ref, kc_ref, vlo_ref, vhi_ref, kn_ref, vn_ref, lq1, lk1, lq2, lk2, dn_ref, o_ref, *,
                        past, lam_init):
    lam = _lam(lq1, lk1, lq2, lk2, lam_init)
    dn = dn_ref[...]
    hw = 2 * DIFF_DQK
    for h in range(DIFF_HEADS):
        cols = slice(h * hw, (h + 1) * hw)
        q = q_ref[:, cols]
        tq = q.shape[0]
        kc = jnp.concatenate(
            [kc_ref[pl.ds(2 * h + mp, past, stride=2 * DIFF_HEADS), :] for mp in range(2)], axis=1).astype(BF16)
        vc = jnp.concatenate(
            [ref[pl.ds(h, past, stride=DIFF_HEADS), :] for ref in (vlo_ref, vhi_ref)], axis=1).astype(BF16)
        sc = _scores(q, kc)
        sn = _scores(q, kn_ref[:, cols])
        m = jnp.maximum(jnp.max(sc, axis=1, keepdims=True), jnp.max(sn, axis=1, keepdims=True))
        pc = jnp.exp2(sc - m)
        pn = jnp.exp2(sn - m)
        l = jnp.sum(pc, axis=1, keepdims=True) + jnp.sum(pn, axis=1, keepdims=True)
        acc = (jnp.dot(pc.astype(BF16), vc, preferred_element_type=F32)
               + jnp.dot(pn.astype(BF16), vn_ref[:, cols], preferred_element_type=F32))
        o_ref[:, cols] = _diff_out(acc[:tq], acc[tq:], l[:tq], l[tq:], lam, dn, lam_init)


def _attn_sample_call(q, kcache, vcache, knew, vnew, lams, dn, past, lam_init):
    rows, width = q.shape
    n_batch = kcache.shape[0]
    t = rows // n_batch
    fixed = lambda b: (0, 0)
    new_spec = pl.BlockSpec((t, width), lambda b: (b, 0))
    vec = pl.BlockSpec((1, DIFF_DQK), fixed)
    return pl.pallas_call(
        functools.partial(_attn_sample_kernel, past=past, lam_init=lam_init),
        grid=(n_batch,),
        in_specs=[new_spec, pl.BlockSpec((None,) + kcache.shape[1:], lambda b: (b, 0, 0)),
                  pl.BlockSpec((None, vcache.shape[1], LANE), lambda b: (b, 0, 0)),
                  pl.BlockSpec((None, vcache.shape[1], LANE), lambda b: (b, 0, 1)), new_spec, new_spec,
                  vec, vec, vec, vec, pl.BlockSpec((1, DIFF_DV), fixed)],
        out_specs=new_spec,
        out_shape=jax.ShapeDtypeStruct((rows, width), BF16),
        compiler_params=_params(("parallel",)),
        name="attn_sample",
    )(q, kcache, vcache, vcache, knew, vnew, *lams, dn)


def _mix_kernel(a_ref, d_ref, ga_ref, gb_ref, wa_ref, wb_ref, o_ref):
    rr = min(MLP_ROWS, o_ref.shape[0])
    for r in range(o_ref.shape[0] // rr):
        rows = slice(r * rr, (r + 1) * rr)
        mix = (ga_ref[rows, :] * jnp.dot(a_ref[rows, :], wa_ref[...], preferred_element_type=F32)
               + gb_ref[rows, :] * jnp.dot(d_ref[rows, :], wb_ref[...], preferred_element_type=F32))
        o_ref[rows, :] = mix.astype(BF16)


def _mix_call(a, d, gates, wa, wb, tm_pref=1024, tn_pref=512):
    rows, k = a.shape
    n = wa.shape[1]
    tm, tn = _tile(rows, tm_pref), _tile(n, tn_pref)
    nj = n // tn
    row_blk = lambda i, j: (i, 0)
    return pl.pallas_call(
        _mix_kernel,
        grid=(rows // tm, nj),
        in_specs=[pl.BlockSpec((tm, k), row_blk), pl.BlockSpec((tm, k), row_blk),
                  pl.BlockSpec((tm, tn), lambda i, j: (i, j)), pl.BlockSpec((tm, tn), lambda i, j: (i, nj + j)),
                  pl.BlockSpec((k, tn), lambda i, j: (0, j)), pl.BlockSpec((k, tn), lambda i, j: (0, j))],
        out_specs=pl.BlockSpec((tm, tn), lambda i, j: (i, j)),
        out_shape=jax.ShapeDtypeStruct((rows, n), BF16),
        compiler_params=_params(("parallel", "arbitrary")),
        name="mix",
    )(a, d, gates, gates, wa, wb)


def _mixout_kernel(mix_ref, wo_ref, res_ref, gpost_ref, gnext_ref, h_ref, u_ref):
    rr = min(EPILOGUE_ROWS, h_ref.shape[0])
    for r in range(h_ref.shape[0] // rr):
        rows = slice(r * rr, (r + 1) * rr)
        t = jnp.dot(mix_ref[rows, :], wo_ref[...], preferred_element_type=F32)
        h = res_ref[rows, :] + _rms(t) * gpost_ref[...]
        h_ref[rows, :] = h
        u_ref[rows, :] = (_rms(h) * gnext_ref[...]).astype(BF16)


def _mixout_call(mix, wo, res, gpost, gnext, tm_pref=512):
    rows, dm = res.shape
    tm = _tile(rows, tm_pref)
    row_blk = lambda i: (i, 0)
    vec = pl.BlockSpec((1, dm), lambda i: (0, 0))
    return pl.pallas_call(
        _mixout_kernel,
        grid=(rows // tm,),
        in_specs=[pl.BlockSpec((tm, mix.shape[1]), row_blk), pl.BlockSpec(wo.shape, lambda i: (0, 0)),
                  pl.BlockSpec((tm, dm), row_blk), vec, vec],
        out_specs=(pl.BlockSpec((tm, dm), row_blk), pl.BlockSpec((tm, dm), row_blk)),
        out_shape=(jax.ShapeDtypeStruct((rows, dm), F32), jax.ShapeDtypeStruct((rows, dm), BF16)),
        compiler_params=_params(("parallel",)),
        name="mixout",
    )(mix, wo, res, gpost, gnext)


def _ffn_kernel(res_ref, u_ref, w1_ref, w2_ref, gpost_ref, o_ref):
    j = pl.program_id(1)

    @pl.when(j == 0)
    def _():
        o_ref[...] = jnp.zeros(o_ref.shape, F32)

    def step(last):
        rr = min(MLP_ROWS, o_ref.shape[0])
        for r in range(o_ref.shape[0] // rr):
            rows = slice(r * rr, (r + 1) * rr)
            mid = jnp.square(jnp.maximum(jnp.dot(u_ref[rows, :], w1_ref[...], preferred_element_type=F32), 0.0))
            acc = o_ref[rows, :] + jnp.dot(mid.astype(BF16), w2_ref[...], preferred_element_type=F32)
            o_ref[rows, :] = res_ref[rows, :] + _rms(acc) * gpost_ref[...] if last else acc

    @pl.when(j < pl.num_programs(1) - 1)
    def _():
        step(False)

    @pl.when(j == pl.num_programs(1) - 1)
    def _():
        step(True)


def _ffn_call(h, u, w1, w2, gpost, tm_pref=512, tf_pref=2048):
    rows, dm = h.shape
    dff = w1.shape[1]
    tm, tf = _tile(rows, tm_pref), _tile(dff, tf_pref)
    row_blk = lambda i, j: (i, 0)
    vec = pl.BlockSpec((1, dm), lambda i, j: (0, 0))
    return pl.pallas_call(
        _ffn_kernel,
        grid=(rows // tm, dff // tf),
        in_specs=[pl.BlockSpec((tm, dm), row_blk), pl.BlockSpec((tm, dm), row_blk),
                  pl.BlockSpec((dm, tf), lambda i, j: (0, j)), pl.BlockSpec((tf, dm), lambda i, j: (j, 0)), vec],
        out_specs=pl.BlockSpec((tm, dm), row_blk),
        out_shape=jax.ShapeDtypeStruct((rows, dm), F32),
        compiler_params=_params(("parallel", "arbitrary")),
        name="ffn",
    )(h, u, w1, w2, gpost)


def _rope_tables(pos):
    inv_freq = jnp.power(ROPE_THETA, -jnp.arange(0, ROPE_DIM, 2, dtype=F32) / ROPE_DIM)
    ang = pos[:, None] * inv_freq[None, :]
    cos, sin = jnp.cos(ang), jnp.sin(ang)
    n = pos.shape[0]
    pad = LANE - ROPE_DIM
    return (jnp.concatenate([cos, cos, jnp.ones((n, pad), F32)], axis=1),
            jnp.concatenate([-sin, sin, jnp.zeros((n, pad), F32)], axis=1))


def kernel(x_prompt, x_sample, cache_k, cache_v, state_gla, meta, norm_mix_pre, w_in, w_gla_a2, b_gla_a, gla_norm, diff_lq1, diff_lk1, diff_lq2, diff_lk2, diff_norm, w_br_gla, w_br_diff, w_o, norm_mix_post, norm_ffn_pre, w_ff1, w_ff2, norm_ffn_post):
    n_b, seq, dm = x_prompt.shape
    n_db, dec_seq, _ = x_sample.shape
    past = cache_k.shape[2]
    assert w_in.shape[0] == 1, "single-layer step only"
    assert dec_seq == CHUNK and seq % CHUNK == 0 and meta.shape[0] == N_META
    lam_init = 0.8 - 0.6 * math.exp(-0.3 * 0)

    sizes = (GLA_HEADS * GLA_DK, GLA_HEADS * GLA_DK, GLA_HEADS * GLA_DV, GATE_RANK, GLA_HEADS * GLA_DV,
             2 * DIFF_HEADS * DIFF_DQK, 2 * DIFF_HEADS * DIFF_DQK, DIFF_HEADS * DIFF_DV, dm, dm)
    offs = [0]
    for s in sizes:
        offs.append(offs[-1] + s)
    wi = w_in[0]
    col = lambda i: wi[:, offs[i]:offs[i + 1]]
    w_qk = jnp.concatenate([col(0), col(1)], axis=1).astype(BF16)
    w_vr = jnp.concatenate([col(2), col(4)], axis=1).astype(BF16)
    w_alow = jnp.pad(col(3), ((0, 0), (0, LANE - GATE_RANK))).astype(BF16)
    w_a2 = jnp.pad(w_gla_a2[0], ((0, LANE - GATE_RANK), (0, 0))).astype(BF16)
    w_q, w_k, w_v = col(5).astype(BF16), col(6).astype(BF16), col(7).astype(BF16)
    w_gate = jnp.concatenate([col(8), col(9)], axis=1).astype(BF16)
    b_a = b_gla_a
    g_pre = norm_mix_pre
    wa, wb, wo = w_br_gla[0].astype(BF16), w_br_diff[0].astype(BF16), w_o[0].astype(BF16)
    w1, w2 = w_ff1[0].astype(BF16), w_ff2[0].astype(BF16)
    lams = (diff_lq1, diff_lk1, diff_lq2, diff_lk2)

    xp = x_prompt.reshape(n_b * seq, dm)
    xs = x_sample.reshape(n_db * dec_seq, dm)
    xm = jnp.concatenate([jnp.zeros((META_LO, dm), x_prompt.dtype), meta.astype(x_prompt.dtype)], axis=0)
    cos_p, sin_p = _rope_tables(jnp.arange(seq, dtype=F32) + N_META)
    cos_s, sin_s = _rope_tables(jnp.tile(jnp.arange(dec_seq, dtype=F32) + past, n_db))
    cos_m, sin_m = _rope_tables(jnp.arange(META_ROWS, dtype=F32) - META_LO)

    def project(x, cos, sin, with_q, lead_rows=0, n_lead=1):
        u = _norm_call(x, g_pre)
        zqk, la = _proj_call("gla", u, w_qk, (w_alow, w_a2, b_a))
        zvr = _proj_call("cast", u, w_vr, tn_pref=1024)
        kf, kb = _proj_call("k", u, w_k, (cos, sin), tm_pref=512, tn_pref=w_k.shape[1],
                            lead_rows=lead_rows, n_lead=n_lead)
        vf, vb = _proj_call("v", u, w_v, lead_rows=lead_rows, n_lead=n_lead)
        if not with_q:
            return zqk, zvr, la, kf, kb, vf, vb
        q = _proj_call("q", u, w_q, (cos, sin))
        gates = _proj_call("gate", u, w_gate, tn_pref=1024)
        return zqk, zvr, la, kf, kb, vf, vb, q, gates

    zqk_m, zvr_m, la_m, kf_m, kb_m, vf_m, vb_m = project(xm, cos_m, sin_m, False)
    zero_state = jnp.zeros((1, GLA_HEADS, GLA_DK, GLA_DV), F32)
    _, s_meta = _gla_call(zqk_m, zvr_m, la_m, zero_state, gla_norm, 1)

    zqk, zvr, la, kf_p, kb_p, vf_p, vb_p, q_p, gates_p = project(xp, cos_p, sin_p, True, N_META, n_b)
    a_p, s_p = _gla_call(zqk, zvr, la, s_meta, gla_norm, n_b)
    key_pad = ((0, LANE - META_ROWS), (0, 0))
    d_p = _attn_prompt_call(q_p, kb_p, vb_p, jnp.pad(kb_m, key_pad), jnp.pad(vb_m, key_pad), lams, diff_norm,
                            n_b, lam_init)
    h_p, u_p = _mixout_call(_mix_call(a_p, d_p, gates_p, wa, wb), wo, xp, norm_mix_post, norm_ffn_pre)
    y_p = _ffn_call(h_p, u_p, w1, w2, norm_ffn_post)

    zqk, zvr, la, kf_s, kb_s, vf_s, vb_s, q_s, gates_s = project(xs, cos_s, sin_s, True)
    a_s, s_s = _gla_call(zqk, zvr, la, state_gla[0].astype(F32), gla_norm, n_db)
    kc = cache_k[0].reshape(n_db, past * DIFF_HEADS * 2, DIFF_DQK)
    vc = cache_v[0].reshape(n_db, past * DIFF_HEADS, DIFF_DV)
    d_s = _attn_sample_call(q_s, kc, vc, kb_s, vb_s, lams, diff_norm, past, lam_init)
    h_s, u_s = _mixout_call(_mix_call(a_s, d_s, gates_s, wa, wb), wo, xs, norm_mix_post, norm_ffn_pre)
    y_s = _ffn_call(h_s, u_s, w1, w2, norm_ffn_post)

    def with_meta(f_meta, f_real):
        per_frame = f_meta.shape[0] // META_ROWS
        m = f_meta[None, META_LO * per_frame:]
        return f_real.reshape(n_b, -1, f_real.shape[1]).at[:, :N_META * per_frame].set(
            jnp.broadcast_to(m, (n_b,) + m.shape[1:]))

    k_shape = (DIFF_HEADS, 2, DIFF_DQK)
    v_shape = (DIFF_HEADS, DIFF_DV)
    return (y_p.reshape(n_b, seq, dm),
            y_s.reshape(n_db, dec_seq, dm),
            with_meta(kf_m, kf_p).reshape((1, n_b, N_META + seq) + k_shape),
            with_meta(vf_m, vf_p).reshape((1, n_b, N_META + seq) + v_shape),
            s_p[None],
            kf_s.reshape((1, n_db, dec_seq) + k_shape),
            vf_s.reshape((1, n_db, dec_seq) + v_shape),
            s_s[None])
```
